```python
import jax, jax.numpy as jnp
from jax import lax
import numpy as np

D_MODEL = 2048
BATCH = 8
SEQ = 2048
DEPTH = 1
DEC_BATCH = 128
DEC_SEQ = 4
PAST_LEN = 2048
PAGE_SIZE = 128

N_HEADS = 8
HEAD_DIM = 128
N_KV_HEADS = 2
ATTN_WIDTH = N_HEADS * HEAD_DIM
CONV_CH = D_MODEL - ATTN_WIDTH
CONV_WIDTH = 31
IDX_HEADS = 8
IDX_DIM = 64
INDEX_TOPK = 256
Q_BLOCK = 128
ROPE_THETA = 500000.0
ROT_FRACTION = 4
PEER_HEADS = 8
PEER_KEYS = 128
PEER_N = PEER_KEYS * PEER_KEYS
PEER_QDIM = 256
PEER_HALF = PEER_QDIM // 2
PEER_TOPK = 16
PEER_CHUNK = 128
PLE_DIM = 256
EPS = 1e-6

OFF_K = N_HEADS * HEAD_DIM
OFF_V = OFF_K + N_KV_HEADS * HEAD_DIM
OFF_QI = OFF_V + N_KV_HEADS * HEAD_DIM
OFF_KI = OFF_QI + IDX_HEADS * IDX_DIM
OFF_WI = OFF_KI + IDX_DIM
OFF_GLU = OFF_WI + IDX_HEADS
N_IN = OFF_GLU + 2 * CONV_CH

kernel_name = 'hymba_dsa_conformer_peer_step'


def rmsnorm(x, g):
    xf = x.astype(jnp.float32)
    y = xf * lax.rsqrt(jnp.mean(xf * xf, axis=-1, keepdims=True) + EPS)
    return (y * g.astype(jnp.float32)).astype(x.dtype)


def partial_rope(x, pos):
    rot = x.shape[-1] // ROT_FRACTION
    half = rot // 2
    inv = ROPE_THETA ** (-jnp.arange(half, dtype=jnp.float32) * 2.0 / rot)
    ang = pos.astype(jnp.float32)[:, None] * inv[None, :]
    cos = jnp.cos(ang)[:, None, :]
    sin = jnp.sin(ang)[:, None, :]
    xf = x.astype(jnp.float32)
    x1 = xf[..., :half]
    x2 = xf[..., half:rot]
    out = jnp.concatenate([x1 * cos - x2 * sin, x2 * cos + x1 * sin, xf[..., rot:]], axis=-1)
    return out.astype(x.dtype)


def dsa_attend(q, qi, wi, k, v, ki, q_pos, top_k):
    b, nq = q.shape[:2]
    L = k.shape[1]
    s = jnp.einsum('bqhe,ble->bqhl', qi.astype(jnp.float32), ki.astype(jnp.float32)) * (IDX_DIM ** -0.5)
    iscore = jnp.einsum('bqhl,bqh->bql', jax.nn.relu(s), wi.astype(jnp.float32)) * (IDX_HEADS ** -0.5)
    causal = jnp.arange(L)[None, :] <= q_pos[:, None]
    iscore = jnp.where(causal[None], iscore, -jnp.inf)
    _, idx = lax.top_k(iscore, top_k)
    valid = idx <= q_pos[None, :, None]
    k_sel = jax.vmap(lambda kb, ib: kb[ib])(k, idx)
    v_sel = jax.vmap(lambda vb, ib: vb[ib])(v, idx)
    qg = q.reshape(b, nq, N_KV_HEADS, N_HEADS // N_KV_HEADS, HEAD_DIM)
    logits = jnp.einsum('bqgrd,bqkgd->bqgrk', qg, k_sel).astype(jnp.float32) * (HEAD_DIM ** -0.5)
    logits = jnp.where(valid[:, :, None, None, :], logits, -jnp.inf)
    p = jax.nn.softmax(logits, axis=-1).astype(v.dtype)
    out = jnp.einsum('bqgrk,bqkgd->bqgrd', p, v_sel)
    return out.reshape(b, nq, N_HEADS, HEAD_DIM)


def attend_prompt(q, k, v, qi, ki, wi):
    b, s = q.shape[:2]
    n_blocks = s // Q_BLOCK
    top_k = min(INDEX_TOPK, s // 4)

    def block(j):
        s0 = j * Q_BLOCK
        qb = lax.dynamic_slice_in_dim(q, s0, Q_BLOCK, axis=1)
        qib = lax.dynamic_slice_in_dim(qi, s0, Q_BLOCK, axis=1)
        wib = lax.dynamic_slice_in_dim(wi, s0, Q_BLOCK, axis=1)
        q_pos = s0 + jnp.arange(Q_BLOCK)
        return dsa_attend(qb, qib, wib, k, v, ki, q_pos, top_k)

    out = lax.map(block, jnp.arange(n_blocks))
    return out.transpose(1, 0, 2, 3, 4).reshape(b, s, N_HEADS, HEAD_DIM)


def make_sample_attend(k_past, v_past, ki_past):
    def attend(q, k, v, qi, ki, wi):
        k_all = jnp.concatenate([k_past.astype(k.dtype), k], axis=1)
        v_all = jnp.concatenate([v_past.astype(v.dtype), v], axis=1)
        ki_all = jnp.concatenate([ki_past.astype(ki.dtype), ki], axis=1)
        L = k_all.shape[1]
        nq = q.shape[1]
        q_pos = (L - nq) + jnp.arange(nq)
        return dsa_attend(q, qi, wi, k_all, v_all, ki_all, q_pos, min(INDEX_TOPK, L // 4))
    return attend


def conformer_conv(a, prev, conv_w, conv_b, ln_g, ln_b):
    glu = a[..., :CONV_CH] * jax.nn.sigmoid(a[..., CONV_CH:])
    seq = jnp.concatenate([prev.astype(glu.dtype), glu], axis=1)
    y = lax.conv_general_dilated(seq, conv_w[:, None, :].astype(seq.dtype), (1,), 'VALID',
                                 dimension_numbers=('NWC', 'WIO', 'NWC'), feature_group_count=CONV_CH)
    yf = (y + conv_b).astype(jnp.float32)
    mu = jnp.mean(yf, axis=-1, keepdims=True)
    var = jnp.mean(jnp.square(yf - mu), axis=-1, keepdims=True)
    yn = (yf - mu) * lax.rsqrt(var + EPS) * ln_g.astype(jnp.float32) + ln_b.astype(jnp.float32)
    return jax.nn.silu(yn).astype(a.dtype), seq[:, -(CONV_WIDTH - 1):]


def peer_ffn(x, wq, subkeys, u, v):
    shp = x.shape
    xt = x.reshape(-1, D_MODEL)
    t = xt.shape[0]
    tp = -(-t // PEER_CHUNK) * PEER_CHUNK
    xt = jnp.pad(xt, ((0, tp - t), (0, 0)))

    def chunk(xc):
        c = xc.shape[0]
        qh = (xc @ wq).reshape(c, PEER_HEADS, 2, PEER_HALF)
        s = jnp.einsum('chnd,hnkd->chnk', qh, subkeys).astype(jnp.float32)
        s1, i1 = lax.top_k(s[:, :, 0], PEER_TOPK)
        s2, i2 = lax.top_k(s[:, :, 1], PEER_TOPK)
        cand = (s1[..., :, None] + s2[..., None, :]).reshape(c, PEER_HEADS, PEER_TOPK * PEER_TOPK)
        cidx = (i1[..., :, None] * PEER_KEYS + i2[..., None, :]).reshape(c, PEER_HEADS, PEER_TOPK * PEER_TOPK)
        best, sel = lax.top_k(cand, PEER_TOPK)
        eidx = jnp.take_along_axis(cidx, sel, axis=-1).reshape(c, PEER_HEADS * PEER_TOPK)
        g = jax.nn.softmax(best, axis=-1).reshape(c, PEER_HEADS * PEER_TOPK)
        hpre = jnp.einsum('cd,ced->ce', xc, u[eidx]).astype(jnp.float32)
        act = (jax.nn.gelu(hpre, approximate=False) * g).astype(xc.dtype)
        return jnp.einsum('ce,ced->cd', act, v[eidx])

    out = lax.map(chunk, xt.reshape(-1, PEER_CHUNK, D_MODEL))
    return out.reshape(tp, D_MODEL)[:t].reshape(shp)


def layer_forward(h, p_i, pos, attend, conv_prev, lw):
    (attn_norm, w_in, conv_w, conv_b, conv_ln_g, conv_ln_b, w_out, ffn_norm,
     peer_wq, peer_subkeys, peer_u, peer_v, ple_norm, ple_gate, ple_proj) = lw
    b, l = h.shape[:2]
    z = rmsnorm(h, attn_norm) @ w_in
    q, k, v, qi, ki, wi, a = jnp.split(z, [OFF_K, OFF_V, OFF_QI, OFF_KI, OFF_WI, OFF_GLU], axis=-1)
    q = partial_rope(q.reshape(b, l, N_HEADS, HEAD_DIM), pos)
    k = partial_rope(k.reshape(b, l, N_KV_HEADS, HEAD_DIM), pos)
    v = v.reshape(b, l, N_KV_HEADS, HEAD_DIM)
    qi = partial_rope(qi.reshape(b, l, IDX_HEADS, IDX_DIM), pos)
    ki = partial_rope(ki[:, :, None, :], pos)[:, :, 0, :]
    attn = attend(q, k, v, qi, ki, wi).reshape(b, l, ATTN_WIDTH)
    conv_out, conv_state = conformer_conv(a, conv_prev, conv_w, conv_b, conv_ln_g, conv_ln_b)
    h = h + jnp.concatenate([attn, conv_out], axis=-1) @ w_out
    h = h + peer_ffn(rmsnorm(h, ffn_norm), peer_wq, peer_subkeys, peer_u, peer_v)
    gate = jax.nn.sigmoid(rmsnorm(h, ple_norm) @ ple_gate)
    h = h + gate * (p_i @ ple_proj)
    return h, k, v, ki, conv_state


def setup_inputs(seed: int = 0) -> dict:
    key = jax.random.key(seed)
    ks = jax.random.split(key, 32)
    n_pages = PAST_LEN // PAGE_SIZE
    n_used = DEC_BATCH * n_pages
    n_pool = n_used + max(1, n_used // 4)
    f32 = jnp.float32

    def nrm(k, shape, s):
        return jax.random.normal(k, shape, f32) * s

    page_table = jax.random.permutation(ks[6], n_pool)[:n_used].reshape(DEC_BATCH, n_pages).astype(jnp.int32)
    return {
        'x_prompt': nrm(ks[0], (BATCH, SEQ, D_MODEL), 1.0),
        'x_sample': nrm(ks[1], (DEC_BATCH, DEC_SEQ, D_MODEL), 1.0),
        'cache_k': nrm(ks[2], (DEPTH, n_pool, PAGE_SIZE, N_KV_HEADS, HEAD_DIM), 1.0),
        'cache_v': nrm(ks[3], (DEPTH, n_pool, PAGE_SIZE, N_KV_HEADS, HEAD_DIM), 1.0),
        'cache_kidx': nrm(ks[4], (DEPTH, n_pool, PAGE_SIZE, IDX_DIM), 1.0),
        'state_conv': nrm(ks[5], (DEPTH, DEC_BATCH, CONV_WIDTH - 1, CONV_CH), 0.5),
        'page_table': page_table,
        'p_prompt': nrm(ks[7], (DEPTH, BATCH, SEQ, PLE_DIM), 1.0),
        'p_sample': nrm(ks[8], (DEPTH, DEC_BATCH, DEC_SEQ, PLE_DIM), 1.0),
        'attn_norm': 1.0 + nrm(ks[9], (DEPTH, D_MODEL), 0.02),
        'w_in': nrm(ks[10], (DEPTH, D_MODEL, N_IN), D_MODEL ** -0.5),
        'conv_w': nrm(ks[11], (DEPTH, CONV_WIDTH, CONV_CH), CONV_WIDTH ** -0.5),
        'conv_b': nrm(ks[12], (DEPTH, CONV_CH), 0.01),
        'conv_ln_g': 1.0 + nrm(ks[13], (DEPTH, CONV_CH), 0.02),
        'conv_ln_b': nrm(ks[14], (DEPTH, CONV_CH), 0.01),
        'w_out': nrm(ks[15], (DEPTH, D_MODEL, D_MODEL), D_MODEL ** -0.5),
        'ffn_norm': 1.0 + nrm(ks[16], (DEPTH, D_MODEL), 0.02),
        'peer_wq': nrm(ks[17], (DEPTH, D_MODEL, PEER_HEADS * PEER_QDIM), D_MODEL ** -0.5),
        'peer_subkeys': nrm(ks[18], (DEPTH, PEER_HEADS, 2, PEER_KEYS, PEER_HALF), PEER_HALF ** -0.5),
        'peer_u': nrm(ks[19], (DEPTH, PEER_N, D_MODEL), D_MODEL ** -0.5),
        'peer_v': nrm(ks[20], (DEPTH, PEER_N, D_MODEL), PEER_HEADS ** -0.5),
        'ple_norm': 1.0 + nrm(ks[21], (DEPTH, D_MODEL), 0.02),
        'ple_gate': nrm(ks[22], (DEPTH, D_MODEL, D_MODEL), D_MODEL ** -0.5),
        'ple_proj': nrm(ks[23], (DEPTH, PLE_DIM, D_MODEL), PLE_DIM ** -0.5),
        'final_norm': 1.0 + nrm(ks[24], (D_MODEL,), 0.02),
    }


def reference(x_prompt, x_sample, cache_k, cache_v, cache_kidx, state_conv, page_table, p_prompt, p_sample,
              attn_norm, w_in, conv_w, conv_b, conv_ln_g, conv_ln_b, w_out, ffn_norm,
              peer_wq, peer_subkeys, peer_u, peer_v, ple_norm, ple_gate, ple_proj, final_norm):
    n_seq = page_table.shape[0]
    past_len = page_table.shape[1] * cache_k.shape[2]
    pos_p = jnp.arange(x_prompt.shape[1])
    pos_s = past_len + jnp.arange(x_sample.shape[1])
    conv_prev_p = jnp.zeros((x_prompt.shape[0], CONV_WIDTH - 1, CONV_CH), x_prompt.dtype)
    hp, hs = x_prompt, x_sample
    kp_l, vp_l, kip_l, cp_l = [], [], [], []
    ks_l, vs_l, kis_l, cs_l = [], [], [], []
    for i in range(DEPTH):
        lw = (attn_norm[i], w_in[i], conv_w[i], conv_b[i], conv_ln_g[i], conv_ln_b[i], w_out[i], ffn_norm[i],
              peer_wq[i], peer_subkeys[i], peer_u[i], peer_v[i], ple_norm[i], ple_gate[i], ple_proj[i])
        hp, kp, vp, kip, cp = layer_forward(hp, p_prompt[i], pos_p, attend_prompt, conv_prev_p, lw)
        k_past = cache_k[i][page_table].reshape(n_seq, past_len, N_KV_HEADS, HEAD_DIM)
        v_past = cache_v[i][page_table].reshape(n_seq, past_len, N_KV_HEADS, HEAD_DIM)
        ki_past = cache_kidx[i][page_table].reshape(n_seq, past_len, IDX_DIM)
        hs, k_s, v_s, ki_s, c_s = layer_forward(hs, p_sample[i], pos_s, make_sample_attend(k_past, v_past, ki_past),
                                                state_conv[i], lw)
        kp_l.append(kp); vp_l.append(vp); kip_l.append(kip); cp_l.append(cp)
        ks_l.append(k_s); vs_l.append(v_s); kis_l.append(ki_s); cs_l.append(c_s)
    y_prompt = rmsnorm(hp, final_norm)
    y_sample = rmsnorm(hs, final_norm)
    return (y_prompt, y_sample, jnp.stack(kp_l), jnp.stack(vp_l), jnp.stack(kip_l), jnp.stack(cp_l),
            jnp.stack(ks_l), jnp.stack(vs_l), jnp.stack(kis_l), jnp.stack(cs_l))
```

```python
import functools

import numpy as np
import jax
import jax.numpy as jnp
from jax import lax
from jax.experimental import pallas as pl
from jax.experimental.pallas import tpu as pltpu

N_HEADS = 8
HEAD_DIM = 128
N_KV_HEADS = 2
IDX_HEADS = 8
IDX_DIM = 64
INDEX_TOPK = 256
CONV_WIDTH = 31
ROPE_THETA = 500000.0
ROT_FRACTION = 4
PEER_HEADS = 8
PEER_KEYS = 128
PEER_TOPK = 16
EPS = 1e-6
DEC_SEQ_PAD = 8

LANES = 128
VMEM_LIMIT = 56 * 1024 * 1024

F32 = jnp.float32
BF16 = jnp.bfloat16
I32 = jnp.int32
INT_MIN = np.int32(-2 ** 31)
NT_DIMS = (((1,), (1,)), ((), ()))

ATTN_WIDTH = N_HEADS * HEAD_DIM
KV_WIDTH = N_KV_HEADS * HEAD_DIM
QI_WIDTH = IDX_HEADS * IDX_DIM
P_Q = 0
P_K = P_Q + ATTN_WIDTH
P_V = P_K + KV_WIDTH
P_QI = P_V + KV_WIDTH
P_KI = P_QI + QI_WIDTH
P_WI = P_KI + LANES
P_GLU = P_WI + LANES


def _cparams(sem=None):
    return pltpu.CompilerParams(dimension_semantics=sem, vmem_limit_bytes=VMEM_LIMIT)


def _resident(shape):
    nd = len(shape)
    return pl.BlockSpec(shape, lambda *_: (0,) * nd, pipeline_mode=pl.Buffered(1))


def _rmsnorm_rows(x, g):
    ms = jnp.mean(x * x, axis=-1, keepdims=True)
    return x * lax.rsqrt(ms + EPS) * g


def _float_key(x):
    bits = pltpu.bitcast(x + 0.0, I32)
    return bits ^ ((bits >> 31) & np.int32(0x7FFFFFFF))


def _token_tile(t):
    return 256 if t % 256 == 0 else 128


def _inproj_kernel(x_ref, g_ref, w_ref, c128, m128, p128, c64, m64, p64,
                   q_ref, k_ref, v_ref, qi_ref, ki_ref, wi_ref, glu_ref, kb_ref, vb_ref, kib_ref,
                   *, conv_ch):
    xn = _rmsnorm_rows(x_ref[...], g_ref[...]).astype(BF16)

    def mm(lo, hi):
        return jnp.dot(xn, w_ref[:, lo:hi], preferred_element_type=F32)

    def rope(z, c, m, p, half):
        return z * c + pltpu.roll(z, LANES - half, 1) * m + pltpu.roll(z, half, 1) * p

    rot_a = HEAD_DIM // ROT_FRACTION // 2
    rot_b = IDX_DIM // ROT_FRACTION // 2
    ca, ma, pa = c128[...], m128[...], p128[...]
    cb, mb, pb = c64[...], m64[...], p64[...]

    zq = mm(P_Q, P_K)
    for h in range(N_HEADS):
        sl = slice(h * HEAD_DIM, (h + 1) * HEAD_DIM)
        q_ref[:, sl] = (rope(zq[:, sl], ca, ma, pa, rot_a) * (HEAD_DIM ** -0.5)).astype(BF16)
    zk = mm(P_K, P_V)
    for h in range(N_KV_HEADS):
        sl = slice(h * HEAD_DIM, (h + 1) * HEAD_DIM)
        r = rope(zk[:, sl], ca, ma, pa, rot_a)
        k_ref[:, sl] = r
        kb_ref[:, sl] = r.astype(BF16)
    zv = mm(P_V, P_QI)
    v_ref[...] = zv
    vb_ref[...] = zv.astype(BF16)
    zqi = mm(P_QI, P_KI)
    for s in range(QI_WIDTH // LANES):
        sl = slice(s * LANES, (s + 1) * LANES)
        qi_ref[:, sl] = (rope(zqi[:, sl], cb, mb, pb, rot_b) * (IDX_DIM ** -0.5)).astype(BF16)
    r = rope(mm(P_KI, P_WI), cb, mb, pb, rot_b)[:, :IDX_DIM]
    ki_ref[...] = r
    kib_ref[...] = r.astype(BF16)
    wi_ref[...] = mm(P_WI, P_GLU) * (IDX_HEADS ** -0.5)
    za = mm(P_GLU, P_GLU + conv_ch)
    zb = mm(P_GLU + conv_ch, P_GLU + 2 * conv_ch)
    glu_ref[...] = za * jax.nn.sigmoid(zb)


def _inproj(x, gain, w_p, tables, conv_ch):
    t, d = x.shape
    tm = _token_tile(t)
    npad = w_p.shape[1]
    row = lambda w: pl.BlockSpec((tm, w), lambda i: (i, 0))
    out_shapes = [
        jax.ShapeDtypeStruct((t, ATTN_WIDTH), BF16),
        jax.ShapeDtypeStruct((t, KV_WIDTH), F32),
        jax.ShapeDtypeStruct((t, KV_WIDTH), F32),
        jax.ShapeDtypeStruct((t, QI_WIDTH), BF16),
        jax.ShapeDtypeStruct((t, IDX_DIM), F32),
        jax.ShapeDtypeStruct((t, LANES), F32),
        jax.ShapeDtypeStruct((t, conv_ch), F32),
        jax.ShapeDtypeStruct((t, KV_WIDTH), BF16),
        jax.ShapeDtypeStruct((t, KV_WIDTH), BF16),
        jax.ShapeDtypeStruct((t, IDX_DIM), BF16),
    ]
    return pl.pallas_call(
        functools.partial(_inproj_kernel, conv_ch=conv_ch),
        grid=(t // tm,),
        in_specs=[row(d), _resident((1, d)), _resident((d, npad))] + [row(LANES)] * 6,
        out_specs=[row(s.shape[1]) for s in out_shapes],
        out_shape=out_shapes,
        compiler_params=_cparams(("parallel",)),
    )(x, gain, w_p, *tables)


def _attn_prompt_kernel(q_ref, qi_ref, wi_ref, k_ref, v_ref, ki_ref, ones_ref, cm_ref, tri_ref, o_ref,
                        *, top_k):
    tq = q_ref.shape[0]
    seq = k_ref.shape[0]
    n_chunks = seq // LANES
    j = pl.program_id(1)
    qpos = j * tq + lax.broadcasted_iota(I32, (tq, seq), 0)
    kpos = lax.broadcasted_iota(I32, (tq, seq), 1)
    causal = kpos <= qpos

    ki = ki_ref[...]
    wi = wi_ref[...]
    score = jnp.zeros((tq, seq), F32)
    for h in range(IDX_HEADS):
        s = lax.dot_general(qi_ref[:, h * IDX_DIM:(h + 1) * IDX_DIM], ki, NT_DIMS,
                            preferred_element_type=F32)
        score = score + jnp.maximum(s, 0.0) * wi[:, h:h + 1]
    key = jnp.where(causal, _float_key(score), INT_MIN)

    ones = ones_ref[...]

    def count(mask):
        return jnp.dot(jnp.where(mask, 1.0, 0.0).astype(BF16), ones, preferred_element_type=F32)

    kf = float(top_k)

    def bisect(i, t_u):
        cand_u = t_u | (jnp.int32(1) << (31 - i))
        cnt = count(key >= (cand_u ^ INT_MIN)[:, :1])
        return jnp.where(cnt >= kf, cand_u, t_u)

    t_u = lax.fori_loop(0, 32, bisect, jnp.zeros((tq, LANES), I32))
    thr = (t_u ^ INT_MIN)[:, :1]

    need = kf - count(key > thr)
    eq_b = jnp.where(key == thr, 1.0, 0.0).astype(BF16)
    offs = jnp.dot(eq_b, cm_ref[...], preferred_element_type=F32)
    tri = tri_ref[...]
    bias_chunks = []
    for c in range(n_chunks):
        sl = slice(c * LANES, (c + 1) * LANES)
        key_c = key[:, sl]
        prefix = jnp.dot(eq_b[:, sl], tri, preferred_element_type=F32) + offs[:, c:c + 1]
        sel = (key_c > thr) | ((key_c == thr) & (prefix <= need))
        bias_chunks.append(jnp.where(sel & causal[:, sl], 0.0, -jnp.inf))
    bias = jnp.concatenate(bias_chunks, axis=1)

    heads_per_group = N_HEADS // N_KV_HEADS
    for g in range(N_KV_HEADS):
        kg = k_ref[:, g * HEAD_DIM:(g + 1) * HEAD_DIM]
        vg = v_ref[:, g * HEAD_DIM:(g + 1) * HEAD_DIM]
        for r in range(heads_per_group):
            sl = slice((g * heads_per_group + r) * HEAD_DIM, (g * heads_per_group + r + 1) * HEAD_DIM)
            logits = lax.dot_general(q_ref[:, sl], kg, NT_DIMS, preferred_element_type=F32) + bias
            m = jnp.max(logits, axis=-1, keepdims=True)
            p = jnp.exp(logits - m)
            den = jnp.sum(p, axis=-1, keepdims=True)
            o = jnp.dot(p.astype(BF16), vg, preferred_element_type=F32)
            o_ref[:, sl] = (o / den).astype(o_ref.dtype)


def _attn_prompt(q, qi, wi, kb, vb, kib, batch, seq):
    tq = LANES
    n_chunks = seq // LANES
    top_k = min(INDEX_TOPK, seq // 4)
    ones = jnp.ones((seq, LANES), BF16)
    pos_chunk = np.arange(seq)[:, None] // LANES
    chunk_mask = jnp.asarray(pos_chunk < np.arange(LANES)[None, :], BF16)
    tri = jnp.asarray(np.arange(LANES)[:, None] <= np.arange(LANES)[None, :], BF16)
    nq = seq // tq
    qrow = lambda w: pl.BlockSpec((tq, w), lambda b, j: (b * nq + j, 0))
    kvrow = lambda w: pl.BlockSpec((seq, w), lambda b, j: (b, 0))
    del n_chunks
    return pl.pallas_call(
        functools.partial(_attn_prompt_kernel, top_k=top_k),
        grid=(batch, nq),
        in_specs=[qrow(ATTN_WIDTH), qrow(QI_WIDTH), qrow(LANES), kvrow(KV_WIDTH), kvrow(KV_WIDTH), kvrow(IDX_DIM),
                  _resident((seq, LANES)), _resident((seq, LANES)), _resident((LANES, LANES))],
        out_specs=qrow(ATTN_WIDTH),
        out_shape=jax.ShapeDtypeStruct((batch * seq, ATTN_WIDTH), BF16),
        compiler_params=_cparams(("parallel", "parallel")),
    )(q, qi, wi, kb, vb, kib, ones, chunk_mask, tri)


def _attn_sample_kernel(pt_ref, qg_ref, qi_ref, wcol_ref, knew_ref, vnew_ref, kinew_ref, *rest,
                        n_pages, top_k, past, dec_seq):
    del pt_ref
    k_pages = rest[:n_pages]
    v_pages = rest[n_pages:2 * n_pages]
    ki_pages = rest[2 * n_pages:3 * n_pages]
    tri_ref, ones_ref, o_ref, kn_s, vn_s, kin_s = rest[3 * n_pages:]
    page = k_pages[0].shape[0]
    n_chunks = n_pages + 1
    rows = DEC_SEQ_PAD

    for scr, new in ((kn_s, knew_ref), (vn_s, vnew_ref), (kin_s, kinew_ref)):
        scr[...] = jnp.zeros(scr.shape, scr.dtype)
        scr[0:dec_seq, :] = new[...]

    def chunk(pages, scr, c, lo, hi):
        src = pages[c] if c < n_pages else scr
        return src[:, lo:hi].astype(BF16)

    row = lax.broadcasted_iota(I32, (rows, page), 0)
    lane = lax.broadcasted_iota(I32, (rows, page), 1)
    qpos = past + row

    qi = qi_ref[...]
    wcol = wcol_ref[...]
    keys, causals = [], []
    for c in range(n_chunks):
        s = lax.dot_general(qi, chunk(ki_pages, kin_s, c, 0, IDX_DIM), NT_DIMS, preferred_element_type=F32)
        s = jnp.maximum(s, 0.0) * wcol
        score = s[0:rows]
        for h in range(1, IDX_HEADS):
            score = score + s[h * rows:(h + 1) * rows]
        causal_c = (c * page + lane) <= qpos
        causals.append(causal_c)
        keys.append(jnp.where(causal_c, _float_key(score), INT_MIN))

    def count(cmp):
        tot = jnp.zeros((rows, page), F32)
        for c in range(n_chunks):
            tot = tot + jnp.where(cmp(keys[c]), 1.0, 0.0)
        return jnp.sum(tot, axis=1, keepdims=True)

    kf = float(top_k)

    def bisect(i, t_u):
        cand_u = t_u | (jnp.int32(1) << (31 - i))
        cand_s = cand_u ^ INT_MIN
        return jnp.where(count(lambda kc: kc >= cand_s) >= kf, cand_u, t_u)

    t_u = lax.fori_loop(0, 32, bisect, jnp.zeros((rows, 1), I32))
    thr = t_u ^ INT_MIN
    need = kf - count(lambda kc: kc > thr)

    tri = tri_ref[...]
    ones = ones_ref[...]
    offs = jnp.zeros((rows, page), F32)
    biases = []
    for c in range(n_chunks):
        eq_c = keys[c] == thr
        eq_f = jnp.where(eq_c, 1.0, 0.0)
        prefix = jnp.dot(eq_f, tri, preferred_element_type=F32) + offs
        sel = (keys[c] > thr) | (eq_c & (prefix <= need))
        b8 = jnp.where(sel & causals[c], 0.0, -jnp.inf)
        biases.append(jnp.concatenate([b8] * (N_HEADS // N_KV_HEADS), axis=0))
        offs = offs + jnp.dot(eq_f, ones, preferred_element_type=F32)

    heads_per_group = N_HEADS // N_KV_HEADS
    for g in range(N_KV_HEADS):
        qg = qg_ref[g]
        lo, hi = g * HEAD_DIM, (g + 1) * HEAD_DIM
        logits = [lax.dot_general(qg, chunk(k_pages, kn_s, c, lo, hi), NT_DIMS, preferred_element_type=F32)
                  + biases[c] for c in range(n_chunks)]
        m = logits[0]
        for c in range(1, n_chunks):
            m = jnp.maximum(m, logits[c])
        m = jnp.max(m, axis=1, keepdims=True)
        den = jnp.zeros(logits[0].shape, F32)
        o = jnp.zeros((heads_per_group * rows, HEAD_DIM), F32)
        for c in range(n_chunks):
            p = jnp.exp(logits[c] - m)
            den = den + p
            o = o + jnp.dot(p.astype(BF16), chunk(v_pages, vn_s, c, lo, hi), preferred_element_type=F32)
        o = o / jnp.sum(den, axis=1, keepdims=True)
        for r in range(heads_per_group):
            o_ref[g * heads_per_group + r] = o[r * rows:(r + 1) * rows, :]


def _attn_sample(q_s, qi_s, wi_s, k_s, v_s, ki_s, cache_k, cache_v, cache_ki, page_table, dec_seq):
    db, n_pages = page_table.shape
    page = cache_k.shape[1]
    past = n_pages * page
    top_k = min(INDEX_TOPK, (past + dec_seq) // 4)
    hpg = N_HEADS // N_KV_HEADS
    pad = DEC_SEQ_PAD - dec_seq
    qg = q_s.reshape(db, dec_seq, N_KV_HEADS, hpg, HEAD_DIM).transpose(0, 2, 3, 1, 4)
    qg = jnp.pad(qg, ((0, 0), (0, 0), (0, 0), (0, pad), (0, 0))).reshape(db, N_KV_HEADS, hpg * DEC_SEQ_PAD, HEAD_DIM)
    qi = qi_s.reshape(db, dec_seq, IDX_HEADS, IDX_DIM).transpose(0, 2, 1, 3)
    qi = jnp.pad(qi, ((0, 0), (0, 0), (0, pad), (0, 0))).reshape(db, IDX_HEADS * DEC_SEQ_PAD, IDX_DIM)
    wcol = wi_s[:, :IDX_HEADS].reshape(db, dec_seq, IDX_HEADS).transpose(0, 2, 1)
    wcol = jnp.pad(wcol, ((0, 0), (0, 0), (0, pad))).reshape(db, IDX_HEADS * DEC_SEQ_PAD, 1)
    wcol = jnp.broadcast_to(wcol, (db, IDX_HEADS * DEC_SEQ_PAD, page))
    knew = k_s.reshape(db, dec_seq, KV_WIDTH)
    vnew = v_s.reshape(db, dec_seq, KV_WIDTH)
    kinew = ki_s.reshape(db, dec_seq, IDX_DIM)
    tri = jnp.asarray(np.arange(page)[:, None] <= np.arange(page)[None, :], F32)
    ones = jnp.ones((page, page), F32)

    def seq_spec(shape):
        nd = len(shape)
        return pl.BlockSpec((None,) + tuple(shape[1:]), lambda b, pt: (b,) + (0,) * (nd - 1))

    def page_spec(width, p):
        return pl.BlockSpec((None, page, width), lambda b, pt, p=p: (pt[b, p], 0, 0))

    const = lambda shape: pl.BlockSpec(shape, lambda b, pt: (0,) * len(shape))
    in_specs = [seq_spec(a.shape) for a in (qg, qi, wcol, knew, vnew, kinew)]
    in_specs += [page_spec(KV_WIDTH, p) for p in range(n_pages)]
    in_specs += [page_spec(KV_WIDTH, p) for p in range(n_pages)]
    in_specs += [page_spec(IDX_DIM, p) for p in range(n_pages)]
    in_specs += [const((page, page)), const((page, page))]
    out = pl.pallas_call(
        functools.partial(_attn_sample_kernel, n_pages=n_pages, top_k=top_k, past=past, dec_seq=dec_seq),
        grid_spec=pltpu.PrefetchScalarGridSpec(
            num_scalar_prefetch=1,
            grid=(db,),
            in_specs=in_specs,
            out_specs=pl.BlockSpec((None, N_HEADS, DEC_SEQ_PAD, HEAD_DIM), lambda b, pt: (b, 0, 0, 0)),
            scratch_shapes=[pltpu.VMEM((page, KV_WIDTH), F32), pltpu.VMEM((page, KV_WIDTH), F32),
                            pltpu.VMEM((page, IDX_DIM), F32)],
        ),
        out_shape=jax.ShapeDtypeStruct((db, N_HEADS, DEC_SEQ_PAD, HEAD_DIM), F32),
        compiler_params=_cparams(("arbitrary",)),
    )(page_table, qg, qi, wcol, knew, vnew, kinew,
      *([cache_k] * n_pages), *([cache_v] * n_pages), *([cache_ki] * n_pages), tri, ones)
    out = out[:, :, :dec_seq, :].transpose(0, 2, 1, 3)
    return out.reshape(db * dec_seq, ATTN_WIDTH).astype(BF16)


def _ln_swish(y, g, b):
    mu = jnp.mean(y, axis=-1, keepdims=True)
    yc = y - mu
    var = jnp.mean(yc * yc, axis=-1, keepdims=True)
    yn = yc * lax.rsqrt(var + EPS) * g + b
    return yn * jax.nn.sigmoid(yn)


CONV_HALO = 32


def _conv_prompt_kernel(glu_ref, w_ref, cb_ref, g_ref, b_ref, o_ref, st_ref, seq_s, y_s):
    tt, ch = glu_ref.shape
    t = pl.program_id(1)

    @pl.when(t == 0)
    def _():
        seq_s[0:CONV_HALO, :] = jnp.zeros((CONV_HALO, ch), F32)

    @pl.when(t > 0)
    def _():
        seq_s[0:CONV_HALO, :] = seq_s[tt:tt + CONV_HALO, :]

    seq_s[CONV_HALO:CONV_HALO + tt, :] = glu_ref[...]

    rc, cc = 32, 256
    first = CONV_HALO - (CONV_WIDTH - 1)
    for c0 in range(0, ch, cc):
        for r0 in range(0, tt, rc):
            acc = jnp.zeros((rc, cc), F32)
            for j in range(CONV_WIDTH):
                lo = first + r0 + j
                acc = acc + seq_s[lo:lo + rc, c0:c0 + cc] * w_ref[j:j + 1, c0:c0 + cc]
            y_s[r0:r0 + rc, c0:c0 + cc] = acc
    o_ref[...] = _ln_swish(y_s[...] + cb_ref[...], g_ref[...], b_ref[...]).astype(o_ref.dtype)

    @pl.when(t == pl.num_programs(1) - 1)
    def _():
        st_ref[...] = seq_s[CONV_HALO + tt - (CONV_WIDTH - 1):CONV_HALO + tt, :]


def _conv_prompt(glu, conv_w, conv_b, ln_g, ln_b, batch, seq):
    ch = glu.shape[1]
    tt = LANES
    nt = seq // tt
    const = lambda shape: pl.BlockSpec(shape, lambda b, t: (0,) * len(shape))
    return pl.pallas_call(
        _conv_prompt_kernel,
        grid=(batch, nt),
        in_specs=[pl.BlockSpec((tt, ch), lambda b, t: (b * nt + t, 0)),
                  const((CONV_WIDTH, ch)), const((1, ch)), const((1, ch)), const((1, ch))],
        out_specs=[pl.BlockSpec((tt, ch), lambda b, t: (b * nt + t, 0)),
                   pl.BlockSpec((None, CONV_WIDTH - 1, ch), lambda b, t: (b, 0, 0))],
        out_shape=[jax.ShapeDtypeStruct((batch * seq, ch), BF16),
                   jax.ShapeDtypeStruct((batch, CONV_WIDTH - 1, ch), F32)],
        scratch_shapes=[pltpu.VMEM((CONV_HALO + tt, ch), F32), pltpu.VMEM((tt, ch), F32)],
        compiler_params=_cparams(("parallel", "arbitrary")),
    )(glu, conv_w, conv_b, ln_g, ln_b)


def _conv_sample_kernel(glu_ref, prev_ref, wsh_ref, w_ref, cb_ref, g_ref, b_ref, o_ref, st_ref, y_s, *, dec_seq):
    nb = prev_ref.shape[0]
    keep = CONV_WIDTH - 1 - dec_seq
    for b in range(nb):
        prev = prev_ref[b]
        for t in range(dec_seq):
            y = jnp.sum(prev * wsh_ref[t], axis=0, keepdims=True)
            for u in range(t + 1):
                j = CONV_WIDTH - 1 - t + u
                y = y + glu_ref[b * dec_seq + u:b * dec_seq + u + 1, :] * w_ref[j:j + 1, :]
            y_s[b * dec_seq + t:b * dec_seq + t + 1, :] = y
        st_ref[b, 0:keep, :] = prev_ref[b, dec_seq:CONV_WIDTH - 1, :]
        st_ref[b, keep:CONV_WIDTH - 1, :] = glu_ref[b * dec_seq:(b + 1) * dec_seq, :]
    o_ref[...] = _ln_swish(y_s[...] + cb_ref[...], g_ref[...], b_ref[...]).astype(o_ref.dtype)


def _conv_sample(glu, prev, conv_w, conv_b, ln_g, ln_b, dec_seq):
    db = prev.shape[0]
    ch = glu.shape[1]
    nb = 8
    wsh = jnp.stack([jnp.pad(conv_w[:CONV_WIDTH - 1 - t], ((t, 0), (0, 0))) for t in range(dec_seq)])
    const = lambda shape: pl.BlockSpec(shape, lambda i: (0,) * len(shape))
    return pl.pallas_call(
        functools.partial(_conv_sample_kernel, dec_seq=dec_seq),
        grid=(db // nb,),
        in_specs=[pl.BlockSpec((nb * dec_seq, ch), lambda i: (i, 0)),
                  pl.BlockSpec((nb, CONV_WIDTH - 1, ch), lambda i: (i, 0, 0)),
                  const((dec_seq, CONV_WIDTH - 1, ch)), const((CONV_WIDTH, ch)),
                  const((1, ch)), const((1, ch)), const((1, ch))],
        out_specs=[pl.BlockSpec((nb * dec_seq, ch), lambda i: (i, 0)),
                   pl.BlockSpec((nb, CONV_WIDTH - 1, ch), lambda i: (i, 0, 0))],
        out_shape=[jax.ShapeDtypeStruct((db * dec_seq, ch), BF16),
                   jax.ShapeDtypeStruct((db, CONV_WIDTH - 1, ch), F32)],
        scratch_shapes=[pltpu.VMEM((nb * dec_seq, ch), F32)],
        compiler_params=_cparams(("parallel",)),
    )(glu, prev, wsh, conv_w, conv_b, ln_g, ln_b)


def _outproj_kernel(h_ref, a_ref, c_ref, wa_ref, wc_ref, o_ref):
    o_ref[...] = (h_ref[...]
                  + jnp.dot(a_ref[...], wa_ref[...], preferred_element_type=F32)
                  + jnp.dot(c_ref[...], wc_ref[...], preferred_element_type=F32))


def _outproj(h, attn, conv, w_attn, w_conv):
    t, d = h.shape
    tm = _token_tile(t)
    row = lambda w: pl.BlockSpec((tm, w), lambda i: (i, 0))
    return pl.pallas_call(
        _outproj_kernel,
        grid=(t // tm,),
        in_specs=[row(d), row(attn.shape[1]), row(conv.shape[1]), _resident(w_attn.shape), _resident(w_conv.shape)],
        out_specs=row(d),
        out_shape=jax.ShapeDtypeStruct((t, d), F32),
        compiler_params=_cparams(("parallel",)),
    )(h, attn, conv, w_attn, w_conv)


def _stair_table():
    k = PEER_TOPK
    rows = [b for b in range(k)]
    for a in range(1, 8):
        rows += [a * k + b if (a + 1) * (b + 1) <= k else -1 for b in range(8)]
    rows += [a * k for a in range(8, k)]
    return np.asarray(rows, np.int32)


def _top_rows(s, k, ids):
    big = np.int32(2 ** 30)
    out_rows = lax.broadcasted_iota(I32, (k, s.shape[1]), 0)
    vals = jnp.zeros((k, s.shape[1]), F32)
    sel_ids = jnp.zeros((k, s.shape[1]), I32)
    for r in range(k):
        m = jnp.max(s, axis=0, keepdims=True)
        first = jnp.min(jnp.where(s == m, ids, big), axis=0, keepdims=True)
        vals = jnp.where(out_rows == r, m, vals)
        sel_ids = jnp.where(out_rows == r, first, sel_ids)
        s = jnp.where(ids == first, -jnp.inf, s)
    return vals, sel_ids


def _peer_select_kernel(h_ref, g_ref, wqt_ref, sk_ref, flat_ref, xn_ref, eidx_ref, gate_ref):
    c = h_ref.shape[0]
    k = PEER_TOPK
    xn = _rmsnorm_rows(h_ref[...], g_ref[...])
    xn_ref[...] = xn
    qt = lax.dot_general(wqt_ref[...], xn.astype(BF16), NT_DIMS, preferred_element_type=F32)
    key_ids = lax.broadcasted_iota(I32, (PEER_KEYS, c), 0)
    flat = flat_ref[...]
    valid = flat >= 0
    e_rows, g_rows = [], []
    for h in range(PEER_HEADS):
        tops = []
        for n in range(2):
            hn = h * 2 + n
            qhn = qt[hn * PEER_KEYS:(hn + 1) * PEER_KEYS, :].astype(BF16)
            s = jnp.dot(sk_ref[hn], qhn, preferred_element_type=F32)
            tops.append(_top_rows(s, k, key_ids))
        (s1, i1), (s2, i2) = tops
        cv = [s1[0:1] + s2] + [s1[a:a + 1] + s2[0:8] for a in range(1, 8)] + [s1[8:k] + s2[0:1]]
        ce = ([i1[0:1] * PEER_KEYS + i2] + [i1[a:a + 1] * PEER_KEYS + i2[0:8] for a in range(1, 8)]
              + [i1[8:k] * PEER_KEYS + i2[0:1]])
        cand_v = jnp.where(valid, jnp.concatenate(cv, axis=0), -jnp.inf)
        cand_e = jnp.concatenate(ce, axis=0)
        big = np.int32(2 ** 30)
        out_rows = lax.broadcasted_iota(I32, (k, c), 0)
        best = jnp.zeros((k, c), F32)
        best_e = jnp.zeros((k, c), I32)
        for r in range(k):
            m = jnp.max(cand_v, axis=0, keepdims=True)
            first = jnp.min(jnp.where(cand_v == m, flat, big), axis=0, keepdims=True)
            hit = flat == first
            e = jnp.max(jnp.where(hit, cand_e, -1), axis=0, keepdims=True)
            best = jnp.where(out_rows == r, m, best)
            best_e = jnp.where(out_rows == r, e, best_e)
            cand_v = jnp.where(hit, -jnp.inf, cand_v)
        ex = jnp.exp(best - best[0:1])
        g_rows.append(ex / jnp.sum(ex, axis=0, keepdims=True))
        e_rows.append(best_e)
    eidx_ref[...] = jnp.concatenate(e_rows, axis=0).T
    gate_ref[...] = jnp.concatenate(g_rows, axis=0).T


def _peer_select(h, gain, wq_t, subkeys):
    t, d = h.shape
    c = _token_tile(t)
    n_sel = PEER_HEADS * PEER_TOPK
    flat = jnp.asarray(np.broadcast_to(_stair_table()[:, None], (_stair_table().shape[0], c)))
    row = lambda w: pl.BlockSpec((c, w), lambda i: (i, 0))
    return pl.pallas_call(
        _peer_select_kernel,
        grid=(t // c,),
        in_specs=[row(d), _resident((1, d)), _resident(wq_t.shape), _resident(subkeys.shape), _resident(flat.shape)],
        out_specs=[row(d), row(n_sel), row(n_sel)],
        out_shape=[jax.ShapeDtypeStruct((t, d), F32), jax.ShapeDtypeStruct((t, n_sel), I32),
                   jax.ShapeDtypeStruct((t, n_sel), F32)],
        compiler_params=_cparams(("parallel",)),
    )(h, gain, wq_t, subkeys, flat)


SPLIT_ROWS = 16


def _split_rows(x):
    hi = x.astype(BF16).astype(F32)
    r = lax.broadcasted_iota(I32, (SPLIT_ROWS, x.shape[1]), 0)
    return jnp.where(r == 0, hi, jnp.where(r == 1, x - hi, 0.0)).astype(BF16)


def _peer_experts_kernel(eidx_hbm, xn_ref, gate_ref, h_ref, tbl_hbm, o_ref, idx_s, buf, idx_sem, sem):
    tb = xn_ref.shape[0]
    n_sel = gate_ref.shape[1]
    i = pl.program_id(0)

    idx_copy = pltpu.make_async_copy(eidx_hbm.at[pl.ds(i * tb, tb), :], idx_s, idx_sem)
    idx_copy.start()
    idx_copy.wait()

    def row_copy(c, j, slot):
        return pltpu.make_async_copy(tbl_hbm.at[pl.ds(idx_s[c, j], 1), :], buf.at[slot, pl.ds(j, 1), :],
                                     sem.at[slot])

    def issue(c, slot):
        for j in range(n_sel):
            row_copy(c, j, slot).start()

    def wait(slot):
        pltpu.make_async_copy(tbl_hbm.at[pl.ds(0, n_sel), :], buf.at[slot], sem.at[slot]).wait()

    issue(0, 0)

    def token(c, carry):
        slot = c % 2

        @pl.when(c + 1 < tb)
        def _():
            issue(c + 1, 1 - slot)

        wait(slot)
        w = buf[slot]
        u = pltpu.bitcast(w & np.int32(-65536), F32).astype(BF16)
        v = pltpu.bitcast(w << 16, F32).astype(BF16)
        hp = lax.dot_general(_split_rows(xn_ref[pl.ds(c, 1), :]), u, NT_DIMS, preferred_element_type=F32)
        hpre = hp[0:1] + hp[1:2]
        gelu = 0.5 * hpre * (1.0 + lax.erf(hpre * (2.0 ** -0.5)))
        act = gelu * gate_ref[pl.ds(c, 1), :]
        o = jnp.dot(_split_rows(act), v, preferred_element_type=F32)
        o_ref[pl.ds(c, 1), :] = h_ref[pl.ds(c, 1), :] + o[0:1] + o[1:2]
        return carry

    lax.fori_loop(0, tb, token, 0)


def _peer_experts(eidx, xn, gate, h, table):
    t, d = h.shape
    n_sel = eidx.shape[1]
    tb = LANES
    row = lambda w: pl.BlockSpec((tb, w), lambda i: (i, 0))
    return pl.pallas_call(
        _peer_experts_kernel,
        grid=(t // tb,),
        in_specs=[pl.BlockSpec(memory_space=pl.ANY), row(d), row(n_sel), row(d), pl.BlockSpec(memory_space=pl.ANY)],
        out_specs=row(d),
        out_shape=jax.ShapeDtypeStruct((t, d), F32),
        scratch_shapes=[pltpu.SMEM((tb, n_sel), I32), pltpu.VMEM((2, n_sel, d), I32),
                        pltpu.SemaphoreType.DMA(()), pltpu.SemaphoreType.DMA((2,))],
        compiler_params=_cparams(("arbitrary",)),
    )(eidx, xn, gate, h, table)


def _ple_kernel(h_ref, g_ref, wg_ref, p_ref, wp_ref, o_ref):
    h = h_ref[...]
    gate = jax.nn.sigmoid(jnp.dot(_rmsnorm_rows(h, g_ref[...]).astype(BF16), wg_ref[...],
                                  preferred_element_type=F32))
    o_ref[...] = h + gate * jnp.dot(p_ref[...].astype(BF16), wp_ref[...], preferred_element_type=F32)


def _final_norm_kernel(h_ref, g_ref, o_ref):
    o_ref[...] = _rmsnorm_rows(h_ref[...], g_ref[...])


def _ple(h, gain, w_gate, p, w_proj):
    t, d = h.shape
    tm = _token_tile(t)
    row = lambda w: pl.BlockSpec((tm, w), lambda i: (i, 0))
    return pl.pallas_call(
        _ple_kernel,
        grid=(t // tm,),
        in_specs=[row(d), _resident((1, d)), _resident(w_gate.shape), row(p.shape[1]), _resident(w_proj.shape)],
        out_specs=row(d),
        out_shape=jax.ShapeDtypeStruct((t, d), F32),
        compiler_params=_cparams(("parallel",)),
    )(h, gain, w_gate, p, w_proj)


def _final_norm(h, gain):
    t, d = h.shape
    tm = _token_tile(t)
    row = pl.BlockSpec((tm, d), lambda i: (i, 0))
    return pl.pallas_call(
        _final_norm_kernel,
        grid=(t // tm,),
        in_specs=[row, _resident((1, d))],
        out_specs=row,
        out_shape=jax.ShapeDtypeStruct((t, d), F32),
        compiler_params=_cparams(("parallel",)),
    )(h, gain)


def _rope_tables(pos, head_width):
    rot = head_width // ROT_FRACTION
    half = rot // 2
    inv = ROPE_THETA ** (-jnp.arange(half, dtype=F32) * 2.0 / rot)
    ang = pos.astype(F32)[:, None] * inv[None, :]
    cos, sin = jnp.cos(ang), jnp.sin(ang)
    n = pos.shape[0]
    zeros = lambda w: jnp.zeros((n, w), F32)
    c = jnp.concatenate([cos, cos, jnp.ones((n, head_width - rot), F32)], axis=1)
    m = jnp.concatenate([-sin, zeros(head_width - half)], axis=1)
    p = jnp.concatenate([zeros(half), sin, zeros(head_width - rot)], axis=1)
    reps = LANES // head_width
    return tuple(jnp.tile(a, (1, reps)) for a in (c, m, p))


def _pack_w_in(w_in, conv_ch):
    d = w_in.shape[0]
    o_k = ATTN_WIDTH
    o_v = o_k + KV_WIDTH
    o_qi = o_v + KV_WIDTH
    o_ki = o_qi + QI_WIDTH
    o_wi = o_ki + IDX_DIM
    o_glu = o_wi + IDX_HEADS
    z = lambda w: jnp.zeros((d, w), w_in.dtype)
    return jnp.concatenate([
        w_in[:, :o_ki], w_in[:, o_ki:o_wi], z(LANES - IDX_DIM), w_in[:, o_wi:o_glu], z(LANES - IDX_HEADS),
        w_in[:, o_glu:o_glu + 2 * conv_ch]], axis=1).astype(BF16)


def _pack_experts(u, v):
    ub = lax.bitcast_convert_type(u.astype(jnp.bfloat16), jnp.uint16).astype(jnp.uint32)
    vb = lax.bitcast_convert_type(v.astype(jnp.bfloat16), jnp.uint16).astype(jnp.uint32)
    return lax.bitcast_convert_type((ub << 16) | vb, I32)


def kernel(x_prompt, x_sample, cache_k, cache_v, cache_kidx, state_conv, page_table, p_prompt, p_sample, attn_norm, w_in, conv_w, conv_b, conv_ln_g, conv_ln_b, w_out, ffn_norm, peer_wq, peer_subkeys, peer_u, peer_v, ple_norm, ple_gate, ple_proj, final_norm):
    batch, seq, d = x_prompt.shape
    db, dec_seq, _ = x_sample.shape
    depth = attn_norm.shape[0]
    n_pool, page = cache_k.shape[1], cache_k.shape[2]
    n_pages = page_table.shape[1]
    past = n_pages * page
    conv_ch = d - ATTN_WIDTH
    tp = batch * seq
    assert seq % LANES == 0 and tp % LANES == 0 and (tp + db * dec_seq) % LANES == 0
    assert page == LANES and dec_seq <= DEC_SEQ_PAD and db % 8 == 0

    pos = jnp.concatenate([jnp.tile(jnp.arange(seq), batch), jnp.tile(past + jnp.arange(dec_seq), db)])
    tables = _rope_tables(pos, HEAD_DIM) + _rope_tables(pos, IDX_DIM)

    h = jnp.concatenate([x_prompt.reshape(tp, d), x_sample.reshape(db * dec_seq, d)], axis=0)
    outs = [[] for _ in range(8)]
    for i in range(depth):
        w_p = _pack_w_in(w_in[i], conv_ch)
        q, k, v, qi, ki, wi, glu, kb, vb, kib = _inproj(h, attn_norm[i][None], w_p, tables, conv_ch)

        attn_p = _attn_prompt(q[:tp], qi[:tp], wi[:tp], kb[:tp], vb[:tp], kib[:tp], batch, seq)
        attn_s = _attn_sample(q[tp:], qi[tp:], wi[tp:], k[tp:], v[tp:], ki[tp:],
                              cache_k[i].reshape(n_pool, page, KV_WIDTH), cache_v[i].reshape(n_pool, page, KV_WIDTH),
                              cache_kidx[i], page_table, dec_seq)
        row = lambda a: a[None]
        conv_p, st_p = _conv_prompt(glu[:tp], conv_w[i], row(conv_b[i]), row(conv_ln_g[i]), row(conv_ln_b[i]),
                                    batch, seq)
        conv_s, st_s = _conv_sample(glu[tp:], state_conv[i], conv_w[i], row(conv_b[i]), row(conv_ln_g[i]),
                                    row(conv_ln_b[i]), dec_seq)
        w_o = w_out[i].astype(BF16)
        h = _outproj(h, jnp.concatenate([attn_p, attn_s], axis=0), jnp.concatenate([conv_p, conv_s], axis=0),
                     w_o[:ATTN_WIDTH], w_o[ATTN_WIDTH:])

        sk = peer_subkeys[i].reshape(PEER_HEADS * 2, PEER_KEYS, -1).astype(BF16)
        xn, eidx, gate = _peer_select(h, row(ffn_norm[i]), peer_wq[i].T.astype(BF16), sk)
        h = _peer_experts(eidx, xn, gate, h, _pack_experts(peer_u[i], peer_v[i]))

        p_all = jnp.concatenate([p_prompt[i].reshape(tp, -1), p_sample[i].reshape(db * dec_seq, -1)], axis=0)
        h = _ple(h, row(ple_norm[i]), ple_gate[i].astype(BF16), p_all, ple_proj[i].astype(BF16))

        outs[0].append(k[:tp].reshape(batch, seq, N_KV_HEADS, HEAD_DIM))
        outs[1].append(v[:tp].reshape(batch, seq, N_KV_HEADS, HEAD_DIM))
        outs[2].append(ki[:tp].reshape(batch, seq, IDX_DIM))
        outs[3].append(st_p)
        outs[4].append(k[tp:].reshape(db, dec_seq, N_KV_HEADS, HEAD_DIM))
        outs[5].append(v[tp:].reshape(db, dec_seq, N_KV_HEADS, HEAD_DIM))
        outs[6].append(ki[tp:].reshape(db, dec_seq, IDX_DIM))
        outs[7].append(st_s)

    y = _final_norm(h, final_norm[None])
    return (y[:tp].reshape(batch, seq, d), y[tp:].reshape(db, dec_seq, d)) + tuple(jnp.stack(o) for o in outs)
```

```python
import functools

import numpy as np
import jax
import jax.numpy as jnp
from jax import lax
from jax.experimental import pallas as pl
from jax.experimental.pallas import tpu as pltpu

N_HEADS = 8
HEAD_DIM = 128
N_KV_HEADS = 2
IDX_HEADS = 8
IDX_DIM = 64
INDEX_TOPK = 256
CONV_WIDTH = 31
ROPE_THETA = 500000.0
ROT_FRACTION = 4
PEER_HEADS = 8
PEER_KEYS = 128
PEER_TOPK = 16
EPS = 1e-6
DEC_SEQ_PAD = 8

LANES = 128
VMEM_LIMIT = 56 * 1024 * 1024

F32 = jnp.float32
BF16 = jnp.bfloat16
I32 = jnp.int32
INT_MIN = np.int32(-2 ** 31)
NT_DIMS = (((1,), (1,)), ((), ()))

ATTN_WIDTH = N_HEADS * HEAD_DIM
KV_WIDTH = N_KV_HEADS * HEAD_DIM
QI_WIDTH = IDX_HEADS * IDX_DIM
P_Q = 0
P_K = P_Q + ATTN_WIDTH
P_V = P_K + KV_WIDTH
P_QI = P_V + KV_WIDTH
P_KI = P_QI + QI_WIDTH
P_WI = P_KI + LANES
P_GLU = P_WI + LANES


def _cparams(sem=None):
    return pltpu.CompilerParams(dimension_semantics=sem, vmem_limit_bytes=VMEM_LIMIT)


def _resident(shape):
    nd = len(shape)
    return pl.BlockSpec(shape, lambda *_: (0,) * nd, pipeline_mode=pl.Buffered(1))


def _rmsnorm_rows(x, g):
    ms = jnp.mean(x * x, axis=-1, keepdims=True)
    return x * lax.rsqrt(ms + EPS) * g


def _code_to_float(code):
    s = code ^ INT_MIN
    return pltpu.bitcast(s ^ ((s >> 31) & np.int32(0x7FFFFFFF)), F32)


def _kth_largest(count_ge, k, shape):
    def step(i, code):
        cand = code | (jnp.int32(1) << (31 - i))
        return jnp.where(count_ge(_code_to_float(cand)) >= k, cand, code)

    code = lax.fori_loop(0, 32, step, jnp.zeros(shape, I32), unroll=2)
    return jnp.where(code == 0, -jnp.inf, _code_to_float(code))


def _token_tile(t):
    return 256 if t % 256 == 0 else 128


def _inproj_kernel(x_ref, g_ref, w_ref, c128, m128, p128, c64, m64, p64,
                   q_ref, k_ref, v_ref, qi_ref, ki_ref, wi_ref, glu_ref, kb_ref, vb_ref, kib_ref,
                   *, conv_ch):
    xn = _rmsnorm_rows(x_ref[...], g_ref[...]).astype(BF16)

    def mm(lo, hi):
        return jnp.dot(xn, w_ref[:, lo:hi], preferred_element_type=F32)

    def rope(z, c, m, p, half):
        return z * c + pltpu.roll(z, LANES - half, 1) * m + pltpu.roll(z, half, 1) * p

    rot_a = HEAD_DIM // ROT_FRACTION // 2
    rot_b = IDX_DIM // ROT_FRACTION // 2
    ca, ma, pa = c128[...], m128[...], p128[...]
    cb, mb, pb = c64[...], m64[...], p64[...]

    zq = mm(P_Q, P_K)
    for h in range(N_HEADS):
        sl = slice(h * HEAD_DIM, (h + 1) * HEAD_DIM)
        q_ref[:, sl] = (rope(zq[:, sl], ca, ma, pa, rot_a) * (HEAD_DIM ** -0.5)).astype(BF16)
    zk = mm(P_K, P_V)
    for h in range(N_KV_HEADS):
        sl = slice(h * HEAD_DIM, (h + 1) * HEAD_DIM)
        r = rope(zk[:, sl], ca, ma, pa, rot_a)
        k_ref[:, sl] = r
        kb_ref[:, sl] = r.astype(BF16)
    zv = mm(P_V, P_QI)
    v_ref[...] = zv
    vb_ref[...] = zv.astype(BF16)
    zqi = mm(P_QI, P_KI)
    for s in range(QI_WIDTH // LANES):
        sl = slice(s * LANES, (s + 1) * LANES)
        qi_ref[:, sl] = (rope(zqi[:, sl], cb, mb, pb, rot_b) * (IDX_DIM ** -0.5)).astype(BF16)
    r = rope(mm(P_KI, P_WI), cb, mb, pb, rot_b)[:, :IDX_DIM]
    ki_ref[...] = r
    kib_ref[...] = r.astype(BF16)
    wi_ref[...] = mm(P_WI, P_GLU) * (IDX_HEADS ** -0.5)
    za = mm(P_GLU, P_GLU + conv_ch)
    zb = mm(P_GLU + conv_ch, P_GLU + 2 * conv_ch)
    glu_ref[...] = za * jax.nn.sigmoid(zb)


def _inproj(x, gain, w_p, tables, conv_ch):
    t, d = x.shape
    tm = _token_tile(t)
    npad = w_p.shape[1]
    row = lambda w: pl.BlockSpec((tm, w), lambda i: (i, 0))
    out_shapes = [
        jax.ShapeDtypeStruct((t, ATTN_WIDTH), BF16),
        jax.ShapeDtypeStruct((t, KV_WIDTH), F32),
        jax.ShapeDtypeStruct((t, KV_WIDTH), F32),
        jax.ShapeDtypeStruct((t, QI_WIDTH), BF16),
        jax.ShapeDtypeStruct((t, IDX_DIM), F32),
        jax.ShapeDtypeStruct((t, LANES), F32),
        jax.ShapeDtypeStruct((t, conv_ch), F32),
        jax.ShapeDtypeStruct((t, KV_WIDTH), BF16),
        jax.ShapeDtypeStruct((t, KV_WIDTH), BF16),
        jax.ShapeDtypeStruct((t, IDX_DIM), BF16),
    ]
    return pl.pallas_call(
        functools.partial(_inproj_kernel, conv_ch=conv_ch),
        grid=(t // tm,),
        in_specs=[row(d), _resident((1, d)), _resident((d, npad))] + [row(LANES)] * 6,
        out_specs=[row(s.shape[1]) for s in out_shapes],
        out_shape=out_shapes,
        compiler_params=_cparams(("parallel",)),
    )(x, gain, w_p, *tables)


def _attn_prompt_tile(q_ref, qi_ref, wi_ref, k_ref, v_ref, ki_ref, ones_ref, tri_ref, o_ref, *, top_k, span):
    tq = q_ref.shape[0]
    n_chunks = span // LANES
    j = pl.program_id(1)
    qpos = j * tq + lax.broadcasted_iota(I32, (tq, LANES), 0)
    lane = lax.broadcasted_iota(I32, (tq, LANES), 1)

    ki = ki_ref[0:span, :]
    wi = wi_ref[...]
    score = jnp.zeros((tq, span), F32)
    for h in range(IDX_HEADS):
        s = lax.dot_general(qi_ref[:, h * IDX_DIM:(h + 1) * IDX_DIM], ki, NT_DIMS,
                            preferred_element_type=F32)
        score = score + jnp.maximum(s, 0.0) * wi[:, h:h + 1]
    causal = [(c * LANES + lane) <= qpos for c in range(n_chunks)]
    keys = [jnp.where(causal[c], score[:, c * LANES:(c + 1) * LANES], -jnp.inf) for c in range(n_chunks)]

    ones = ones_ref[...]

    def lane_total(x):
        return jnp.dot(x.astype(BF16), ones, preferred_element_type=F32)

    def count(cmp):
        cnt = jnp.zeros((tq, LANES), F32)
        for c in range(n_chunks):
            cnt = cnt + jnp.where(cmp(keys[c]), 1.0, 0.0)
        return lane_total(cnt)

    kf = float(top_k)
    thr = _kth_largest(lambda t: count(lambda kc: kc >= t), kf, (tq, LANES))

    need = kf - count(lambda kc: kc > thr)
    tri = tri_ref[...]
    offs = jnp.zeros((tq, LANES), F32)
    bias_chunks = []
    for c in range(n_chunks):
        eq_c = keys[c] == thr
        eq_b = jnp.where(eq_c, 1.0, 0.0).astype(BF16)
        prefix = jnp.dot(eq_b, tri, preferred_element_type=F32) + offs
        sel = (keys[c] > thr) | (eq_c & (prefix <= need))
        bias_chunks.append(jnp.where(sel & causal[c], 0.0, -jnp.inf))
        offs = offs + jnp.dot(eq_b, ones, preferred_element_type=F32)
    bias = jnp.concatenate(bias_chunks, axis=1)

    heads_per_group = N_HEADS // N_KV_HEADS
    for g in range(N_KV_HEADS):
        kg = k_ref[0:span, g * HEAD_DIM:(g + 1) * HEAD_DIM]
        vg = v_ref[0:span, g * HEAD_DIM:(g + 1) * HEAD_DIM]
        for r in range(heads_per_group):
            sl = slice((g * heads_per_group + r) * HEAD_DIM, (g * heads_per_group + r + 1) * HEAD_DIM)
            logits = lax.dot_general(q_ref[:, sl], kg, NT_DIMS, preferred_element_type=F32) + bias
            m = jnp.max(logits, axis=-1, keepdims=True)
            p = jnp.exp(logits - m)
            den = jnp.sum(p, axis=-1, keepdims=True)
            o = jnp.dot(p.astype(BF16), vg, preferred_element_type=F32)
            o_ref[:, sl] = (o / den).astype(o_ref.dtype)


def _attn_prompt_kernel(*refs, top_k, n_cls):
    tq = refs[0].shape[0]
    nq = refs[3].shape[0] // tq
    width = nq // n_cls
    j = pl.program_id(1)
    for cls in range(n_cls):
        @pl.when(j // width == cls)
        def _(cls=cls):
            _attn_prompt_tile(*refs, top_k=top_k, span=(cls + 1) * width * tq)


def _attn_prompt(q, qi, wi, kb, vb, kib, batch, seq):
    tq = LANES
    top_k = min(INDEX_TOPK, seq // 4)
    ones = jnp.ones((LANES, LANES), BF16)
    tri = jnp.asarray(np.arange(LANES)[:, None] <= np.arange(LANES)[None, :], BF16)
    nq = seq // tq
    n_cls = 4 if nq % 4 == 0 else (2 if nq % 2 == 0 else 1)
    qrow = lambda w: pl.BlockSpec((tq, w), lambda b, j: (b * nq + j, 0))
    kvrow = lambda w: pl.BlockSpec((seq, w), lambda b, j: (b, 0))
    return pl.pallas_call(
        functools.partial(_attn_prompt_kernel, top_k=top_k, n_cls=n_cls),
        grid=(batch, nq),
        in_specs=[qrow(ATTN_WIDTH), qrow(QI_WIDTH), qrow(LANES), kvrow(KV_WIDTH), kvrow(KV_WIDTH), kvrow(IDX_DIM),
                  _resident((LANES, LANES)), _resident((LANES, LANES))],
        out_specs=qrow(ATTN_WIDTH),
        out_shape=jax.ShapeDtypeStruct((batch * seq, ATTN_WIDTH), BF16),
        compiler_params=_cparams(("parallel", "parallel")),
    )(q, qi, wi, kb, vb, kib, ones, tri)


def _attn_sample_kernel(pt_ref, qg_ref, qi_ref, wcol_ref, knew_ref, vnew_ref, kinew_ref, *rest,
                        n_pages, top_k, past, dec_seq):
    del pt_ref
    k_pages = rest[:n_pages]
    v_pages = rest[n_pages:2 * n_pages]
    ki_pages = rest[2 * n_pages:3 * n_pages]
    tri_ref, ones_ref, o_ref, kn_s, vn_s, kin_s = rest[3 * n_pages:]
    page = k_pages[0].shape[0]
    n_chunks = n_pages + 1
    rows = DEC_SEQ_PAD

    for scr, new in ((kn_s, knew_ref), (vn_s, vnew_ref), (kin_s, kinew_ref)):
        scr[...] = jnp.zeros(scr.shape, scr.dtype)
        scr[0:dec_seq, :] = new[...]

    def chunk(pages, scr, c, g):
        if c < n_pages:
            return pages[c][:, g, :].astype(BF16)
        return scr[:, g * HEAD_DIM:(g + 1) * HEAD_DIM].astype(BF16)

    def ki_chunk(c):
        return (ki_pages[c] if c < n_pages else kin_s)[...].astype(BF16)

    row = lax.broadcasted_iota(I32, (rows, page), 0)
    lane = lax.broadcasted_iota(I32, (rows, page), 1)
    qpos = past + row

    qi = qi_ref[...]
    wcol = wcol_ref[...]
    keys, causals = [], []
    for c in range(n_chunks):
        s = lax.dot_general(qi, ki_chunk(c), NT_DIMS, preferred_element_type=F32)
        s = jnp.maximum(s, 0.0) * wcol
        score = s[0:rows]
        for h in range(1, IDX_HEADS):
            score = score + s[h * rows:(h + 1) * rows]
        causal_c = (c * page + lane) <= qpos
        causals.append(causal_c)
        keys.append(jnp.where(causal_c, score, -jnp.inf))

    def count(cmp):
        tot = jnp.zeros((rows, page), F32)
        for c in range(n_chunks):
            tot = tot + jnp.where(cmp(keys[c]), 1.0, 0.0)
        return jnp.broadcast_to(jnp.sum(tot, axis=1, keepdims=True), (rows, page))

    kf = float(top_k)
    thr = _kth_largest(lambda t: count(lambda kc: kc >= t), kf, (rows, page))
    need = kf - count(lambda kc: kc > thr)

    tri = tri_ref[...]
    ones = ones_ref[...]
    offs = jnp.zeros((rows, page), F32)
    biases = []
    for c in range(n_chunks):
        eq_c = keys[c] == thr
        eq_f = jnp.where(eq_c, 1.0, 0.0)
        prefix = jnp.dot(eq_f, tri, preferred_element_type=F32) + offs
        sel = (keys[c] > thr) | (eq_c & (prefix <= need))
        b8 = jnp.where(sel & causals[c], 0.0, -jnp.inf)
        biases.append(jnp.concatenate([b8] * (N_HEADS // N_KV_HEADS), axis=0))
        offs = offs + jnp.dot(eq_f, ones, preferred_element_type=F32)

    heads_per_group = N_HEADS // N_KV_HEADS
    for g in range(N_KV_HEADS):
        qg = qg_ref[g]
        logits = [lax.dot_general(qg, chunk(k_pages, kn_s, c, g), NT_DIMS, preferred_element_type=F32)
                  + biases[c] for c in range(n_chunks)]
        m = logits[0]
        for c in range(1, n_chunks):
            m = jnp.maximum(m, logits[c])
        m = jnp.max(m, axis=1, keepdims=True)
        den = jnp.zeros(logits[0].shape, F32)
        o = jnp.zeros((heads_per_group * rows, HEAD_DIM), F32)
        for c in range(n_chunks):
            p = jnp.exp(logits[c] - m)
            den = den + p
            o = o + jnp.dot(p.astype(BF16), chunk(v_pages, vn_s, c, g), preferred_element_type=F32)
        o = o / jnp.sum(den, axis=1, keepdims=True)
        for r in range(heads_per_group):
            o_ref[g * heads_per_group + r] = o[r * rows:(r + 1) * rows, :]


def _attn_sample(q_s, qi_s, wi_s, k_s, v_s, ki_s, cache_k, cache_v, cache_ki, layer, page_table, dec_seq):
    db, n_pages = page_table.shape
    page = cache_k.shape[2]
    past = n_pages * page
    top_k = min(INDEX_TOPK, (past + dec_seq) // 4)
    hpg = N_HEADS // N_KV_HEADS
    pad = DEC_SEQ_PAD - dec_seq
    qg = q_s.reshape(db, dec_seq, N_KV_HEADS, hpg, HEAD_DIM).transpose(0, 2, 3, 1, 4)
    qg = jnp.pad(qg, ((0, 0), (0, 0), (0, 0), (0, pad), (0, 0))).reshape(db, N_KV_HEADS, hpg * DEC_SEQ_PAD, HEAD_DIM)
    qi = qi_s.reshape(db, dec_seq, IDX_HEADS, IDX_DIM).transpose(0, 2, 1, 3)
    qi = jnp.pad(qi, ((0, 0), (0, 0), (0, pad), (0, 0))).reshape(db, IDX_HEADS * DEC_SEQ_PAD, IDX_DIM)
    wcol = wi_s[:, :IDX_HEADS].reshape(db, dec_seq, IDX_HEADS).transpose(0, 2, 1)
    wcol = jnp.pad(wcol, ((0, 0), (0, 0), (0, pad))).reshape(db, IDX_HEADS * DEC_SEQ_PAD, 1)
    wcol = jnp.broadcast_to(wcol, (db, IDX_HEADS * DEC_SEQ_PAD, page))
    knew = k_s.reshape(db, dec_seq, KV_WIDTH)
    vnew = v_s.reshape(db, dec_seq, KV_WIDTH)
    kinew = ki_s.reshape(db, dec_seq, IDX_DIM)
    tri = jnp.asarray(np.arange(page)[:, None] <= np.arange(page)[None, :], F32)
    ones = jnp.ones((page, page), F32)

    def seq_spec(shape):
        nd = len(shape)
        return pl.BlockSpec((None,) + tuple(shape[1:]), lambda b, pt: (b,) + (0,) * (nd - 1))

    def page_spec(tail, p):
        return pl.BlockSpec((None, None, page) + tail,
                            lambda b, pt: (layer, pt[b, p], 0) + (0,) * len(tail))

    kv_specs = [page_spec((N_KV_HEADS, HEAD_DIM), p) for p in range(n_pages)]

    const = lambda shape: pl.BlockSpec(shape, lambda b, pt: (0,) * len(shape))
    in_specs = [seq_spec(a.shape) for a in (qg, qi, wcol, knew, vnew, kinew)]
    in_specs += kv_specs + kv_specs + [page_spec((IDX_DIM,), p) for p in range(n_pages)]
    in_specs += [const((page, page)), const((page, page))]
    out = pl.pallas_call(
        functools.partial(_attn_sample_kernel, n_pages=n_pages, top_k=top_k, past=past, dec_seq=dec_seq),
        grid_spec=pltpu.PrefetchScalarGridSpec(
            num_scalar_prefetch=1,
            grid=(db,),
            in_specs=in_specs,
            out_specs=pl.BlockSpec((None, N_HEADS, DEC_SEQ_PAD, HEAD_DIM), lambda b, pt: (b, 0, 0, 0)),
            scratch_shapes=[pltpu.VMEM((page, KV_WIDTH), F32), pltpu.VMEM((page, KV_WIDTH), F32),
                            pltpu.VMEM((page, IDX_DIM), F32)],
        ),
        out_shape=jax.ShapeDtypeStruct((db, N_HEADS, DEC_SEQ_PAD, HEAD_DIM), F32),
        compiler_params=_cparams(("arbitrary",)),
    )(page_table, qg, qi, wcol, knew, vnew, kinew,
      *([cache_k] * len(kv_specs)), *([cache_v] * len(kv_specs)), *([cache_ki] * n_pages), tri, ones)
    out = out[:, :, :dec_seq, :].transpose(0, 2, 1, 3)
    return out.reshape(db * dec_seq, ATTN_WIDTH).astype(BF16)


def _ln_swish(y, g, b):
    mu = jnp.mean(y, axis=-1, keepdims=True)
    yc = y - mu
    var = jnp.mean(yc * yc, axis=-1, keepdims=True)
    yn = yc * lax.rsqrt(var + EPS) * g + b
    return yn * jax.nn.sigmoid(yn)


CONV_HALO = 32


def _conv_prompt_kernel(glu_ref, w_ref, cb_ref, g_ref, b_ref, o_ref, st_ref, seq_s, y_s):
    tt, ch = glu_ref.shape
    t = pl.program_id(1)

    @pl.when(t == 0)
    def _():
        seq_s[0:CONV_HALO, :] = jnp.zeros((CONV_HALO, ch), F32)

    @pl.when(t > 0)
    def _():
        seq_s[0:CONV_HALO, :] = seq_s[tt:tt + CONV_HALO, :]

    seq_s[CONV_HALO:CONV_HALO + tt, :] = glu_ref[...]

    rc, cc = 32, 256
    first = CONV_HALO - (CONV_WIDTH - 1)
    for c0 in range(0, ch, cc):
        for r0 in range(0, tt, rc):
            acc = jnp.zeros((rc, cc), F32)
            for j in range(CONV_WIDTH):
                lo = first + r0 + j
                acc = acc + seq_s[lo:lo + rc, c0:c0 + cc] * w_ref[j:j + 1, c0:c0 + cc]
            y_s[r0:r0 + rc, c0:c0 + cc] = acc
    o_ref[...] = _ln_swish(y_s[...] + cb_ref[...], g_ref[...], b_ref[...]).astype(o_ref.dtype)

    @pl.when(t == pl.num_programs(1) - 1)
    def _():
        st_ref[...] = seq_s[CONV_HALO + tt - (CONV_WIDTH - 1):CONV_HALO + tt, :]


def _conv_prompt(glu, conv_w, conv_b, ln_g, ln_b, batch, seq):
    ch = glu.shape[1]
    tt = LANES
    nt = seq // tt
    const = lambda shape: pl.BlockSpec(shape, lambda b, t: (0,) * len(shape))
    return pl.pallas_call(
        _conv_prompt_kernel,
        grid=(batch, nt),
        in_specs=[pl.BlockSpec((tt, ch), lambda b, t: (b * nt + t, 0)),
                  const((CONV_WIDTH, ch)), const((1, ch)), const((1, ch)), const((1, ch))],
        out_specs=[pl.BlockSpec((tt, ch), lambda b, t: (b * nt + t, 0)),
                   pl.BlockSpec((None, CONV_WIDTH - 1, ch), lambda b, t: (b, 0, 0))],
        out_shape=[jax.ShapeDtypeStruct((batch * seq, ch), BF16),
                   jax.ShapeDtypeStruct((batch, CONV_WIDTH - 1, ch), F32)],
        scratch_shapes=[pltpu.VMEM((CONV_HALO + tt, ch), F32), pltpu.VMEM((tt, ch), F32)],
        compiler_params=_cparams(("parallel", "arbitrary")),
    )(glu, conv_w, conv_b, ln_g, ln_b)


def _conv_sample_kernel(glu_ref, prev_ref, wsh_ref, w_ref, cb_ref, g_ref, b_ref, o_ref, st_ref, y_s, *, dec_seq):
    nb = prev_ref.shape[0]
    keep = CONV_WIDTH - 1 - dec_seq
    for b in range(nb):
        prev = prev_ref[b]
        for t in range(dec_seq):
            y = jnp.sum(prev * wsh_ref[t], axis=0, keepdims=True)
            for u in range(t + 1):
                j = CONV_WIDTH - 1 - t + u
                y = y + glu_ref[b * dec_seq + u:b * dec_seq + u + 1, :] * w_ref[j:j + 1, :]
            y_s[b * dec_seq + t:b * dec_seq + t + 1, :] = y
        st_ref[b, 0:keep, :] = prev_ref[b, dec_seq:CONV_WIDTH - 1, :]
        st_ref[b, keep:CONV_WIDTH - 1, :] = glu_ref[b * dec_seq:(b + 1) * dec_seq, :]
    o_ref[...] = _ln_swish(y_s[...] + cb_ref[...], g_ref[...], b_ref[...]).astype(o_ref.dtype)


def _conv_sample(glu, prev, conv_w, conv_b, ln_g, ln_b, dec_seq):
    db = prev.shape[0]
    ch = glu.shape[1]
    nb = 8
    wsh = jnp.stack([jnp.pad(conv_w[:CONV_WIDTH - 1 - t], ((t, 0), (0, 0))) for t in range(dec_seq)])
    const = lambda shape: pl.BlockSpec(shape, lambda i: (0,) * len(shape))
    return pl.pallas_call(
        functools.partial(_conv_sample_kernel, dec_seq=dec_seq),
        grid=(db // nb,),
        in_specs=[pl.BlockSpec((nb * dec_seq, ch), lambda i: (i, 0)),
                  pl.BlockSpec((nb, CONV_WIDTH - 1, ch), lambda i: (i, 0, 0)),
                  const((dec_seq, CONV_WIDTH - 1, ch)), const((CONV_WIDTH, ch)),
                  const((1, ch)), const((1, ch)), const((1, ch))],
        out_specs=[pl.BlockSpec((nb * dec_seq, ch), lambda i: (i, 0)),
                   pl.BlockSpec((nb, CONV_WIDTH - 1, ch), lambda i: (i, 0, 0))],
        out_shape=[jax.ShapeDtypeStruct((db * dec_seq, ch), BF16),
                   jax.ShapeDtypeStruct((db, CONV_WIDTH - 1, ch), F32)],
        scratch_shapes=[pltpu.VMEM((nb * dec_seq, ch), F32)],
        compiler_params=_cparams(("parallel",)),
    )(glu, prev, wsh, conv_w, conv_b, ln_g, ln_b)


def _outproj_kernel(h_ref, a_ref, c_ref, wa_ref, wc_ref, o_ref):
    o_ref[...] = (h_ref[...]
                  + jnp.dot(a_ref[...], wa_ref[...], preferred_element_type=F32)
                  + jnp.dot(c_ref[...], wc_ref[...], preferred_element_type=F32))


def _outproj(h, attn, conv, w_attn, w_conv):
    t, d = h.shape
    tm = _token_tile(t)
    row = lambda w: pl.BlockSpec((tm, w), lambda i: (i, 0))
    return pl.pallas_call(
        _outproj_kernel,
        grid=(t // tm,),
        in_specs=[row(d), row(attn.shape[1]), row(conv.shape[1]), _resident(w_attn.shape), _resident(w_conv.shape)],
        out_specs=row(d),
        out_shape=jax.ShapeDtypeStruct((t, d), F32),
        compiler_params=_cparams(("parallel",)),
    )(h, attn, conv, w_attn, w_conv)


def _stair_table():
    k = PEER_TOPK
    rows = [b for b in range(k)]
    for a in range(1, 8):
        rows += [a * k + b if (a + 1) * (b + 1) <= k else -1 for b in range(8)]
    rows += [a * k for a in range(8, k)]
    return np.asarray(rows, np.int32)


def _top_rows(s, k, ids):
    big = np.int32(2 ** 30)
    out_rows = lax.broadcasted_iota(I32, (k, s.shape[1]), 0)
    vals = jnp.zeros((k, s.shape[1]), F32)
    sel_ids = jnp.zeros((k, s.shape[1]), I32)
    for r in range(k):
        m = jnp.max(s, axis=0, keepdims=True)
        first = jnp.min(jnp.where(s == m, ids, big), axis=0, keepdims=True)
        vals = jnp.where(out_rows == r, m, vals)
        sel_ids = jnp.where(out_rows == r, first, sel_ids)
        s = jnp.where(ids == first, -jnp.inf, s)
    return vals, sel_ids


def _peer_select_kernel(h_ref, g_ref, wqt_ref, sk_ref, flat_ref, xn_ref, eidx_ref, gate_ref):
    c = h_ref.shape[0]
    k = PEER_TOPK
    xn = _rmsnorm_rows(h_ref[...], g_ref[...])
    xn_ref[...] = xn
    qt = lax.dot_general(wqt_ref[...], xn.astype(BF16), NT_DIMS, preferred_element_type=F32)
    key_ids = lax.broadcasted_iota(I32, (PEER_KEYS, c), 0)
    flat = flat_ref[...]
    valid = flat >= 0
    e_rows, g_rows = [], []
    for h in range(PEER_HEADS):
        tops = []
        for n in range(2):
            hn = h * 2 + n
            qhn = qt[hn * PEER_KEYS:(hn + 1) * PEER_KEYS, :].astype(BF16)
            s = jnp.dot(sk_ref[hn], qhn, preferred_element_type=F32)
            tops.append(_top_rows(s, k, key_ids))
        (s1, i1), (s2, i2) = tops
        cv = [s1[0:1] + s2] + [s1[a:a + 1] + s2[0:8] for a in range(1, 8)] + [s1[8:k] + s2[0:1]]
        ce = ([i1[0:1] * PEER_KEYS + i2] + [i1[a:a + 1] * PEER_KEYS + i2[0:8] for a in range(1, 8)]
              + [i1[8:k] * PEER_KEYS + i2[0:1]])
        cand_v = jnp.where(valid, jnp.concatenate(cv, axis=0), -jnp.inf)
        cand_e = jnp.concatenate(ce, axis=0)
        big = np.int32(2 ** 30)
        out_rows = lax.broadcasted_iota(I32, (k, c), 0)
        best = jnp.zeros((k, c), F32)
        best_e = jnp.zeros((k, c), I32)
        for r in range(k):
            m = jnp.max(cand_v, axis=0, keepdims=True)
            first = jnp.min(jnp.where(cand_v == m, flat, big), axis=0, keepdims=True)
            hit = flat == first
            e = jnp.max(jnp.where(hit, cand_e, -1), axis=0, keepdims=True)
            best = jnp.where(out_rows == r, m, best)
            best_e = jnp.where(out_rows == r, e, best_e)
            cand_v = jnp.where(hit, -jnp.inf, cand_v)
        ex = jnp.exp(best - best[0:1])
        g_rows.append(ex / jnp.sum(ex, axis=0, keepdims=True))
        e_rows.append(best_e)
    eidx_ref[...] = jnp.concatenate(e_rows, axis=0).T
    gate_ref[...] = jnp.concatenate(g_rows, axis=0).T


def _peer_select(h, gain, wq_t, subkeys):
    t, d = h.shape
    c = _token_tile(t)
    n_sel = PEER_HEADS * PEER_TOPK
    flat = jnp.asarray(np.broadcast_to(_stair_table()[:, None], (_stair_table().shape[0], c)))
    row = lambda w: pl.BlockSpec((c, w), lambda i: (i, 0))
    return pl.pallas_call(
        _peer_select_kernel,
        grid=(t // c,),
        in_specs=[row(d), _resident((1, d)), _resident(wq_t.shape), _resident(subkeys.shape), _resident(flat.shape)],
        out_specs=[row(d), row(n_sel), row(n_sel)],
        out_shape=[jax.ShapeDtypeStruct((t, d), F32), jax.ShapeDtypeStruct((t, n_sel), I32),
                   jax.ShapeDtypeStruct((t, n_sel), F32)],
        compiler_params=_cparams(("parallel",)),
    )(h, gain, wq_t, subkeys, flat)


EXPERT_AHEAD = 2
EXPERT_SLOTS = 4
EXPERT_TOKEN_BLOCK = 192


def _split_hi_lo(x):
    hi = x.astype(BF16)
    return jnp.concatenate([hi, (x - hi.astype(F32)).astype(BF16)], axis=0)


def _peer_experts_kernel(eidx_hbm, xn_ref, gate_ref, h_ref, tbl_hbm, o_ref, idx_s, *rest):
    bufs = rest[:EXPERT_SLOTS]
    idx_sem, sem = rest[EXPERT_SLOTS:]
    tb, ns, _ = xn_ref.shape
    n_sel = gate_ref.shape[1]
    i = pl.program_id(0)
    cur = i % 2

    def idx_copy(step, slot):
        return pltpu.make_async_copy(eidx_hbm.at[pl.ds(step * tb, tb), :],
                                     idx_s.at[pl.ds(pl.multiple_of(slot * tb, 8), tb), :], idx_sem.at[slot])

    @pl.when(i == 0)
    def _():
        idx_copy(0, 0).start()

    idx_copy(i, cur).wait()

    @pl.when(i + 1 < pl.num_programs(0))
    def _():
        idx_copy(i + 1, 1 - cur).start()

    def issue(c, slot):
        for j in range(n_sel):
            src = tbl_hbm.at[pl.ds(pl.multiple_of(idx_s[cur * tb + c, j] * ns, ns), ns), :]
            pltpu.make_async_copy(src, bufs[slot].at[:, j, :], sem.at[slot]).start(priority=j % 2)

    def wait(slot):
        pltpu.make_async_copy(bufs[slot], bufs[slot], sem.at[slot]).wait()

    for c in range(EXPERT_AHEAD):
        issue(c, c)
    diag = lax.broadcasted_iota(I32, (2 * ns, n_sel), 0) % ns
    row = lax.broadcasted_iota(I32, (ns, n_sel), 0)

    def words(slot):
        return bufs[slot][...].reshape(ns * n_sel, LANES)

    def pre_activation(c, slot):
        u = pltpu.bitcast(words(slot) & np.int32(-65536), F32).astype(BF16)
        y = lax.dot_general(_split_hi_lo(xn_ref[c]), u, NT_DIMS, preferred_element_type=F32)
        acc = jnp.zeros((2 * ns, n_sel), F32)
        for s in range(ns):
            acc = acc + jnp.where(diag == s, y[:, s * n_sel:(s + 1) * n_sel], 0.0)
        return jnp.sum(acc, axis=0, keepdims=True)

    def combine(c, slot, hpre):
        gelu = 0.5 * hpre * (1.0 + lax.erf(hpre * (2.0 ** -0.5)))
        act = jnp.broadcast_to(gelu * gate_ref[pl.ds(c, 1), :], (ns, n_sel))
        a_bd = jnp.concatenate([jnp.where(row == s, act, 0.0) for s in range(ns)], axis=1)
        v = pltpu.bitcast(words(slot) << 16, F32).astype(BF16)
        o = jnp.dot(_split_hi_lo(a_bd), v, preferred_element_type=F32)
        o_ref[c] = h_ref[c] + o[:ns] + o[ns:]

    def region(c, slot, hpre_prev):
        wait(slot)
        issue(jnp.minimum(c + EXPERT_AHEAD, tb - 1), (slot + EXPERT_AHEAD) % EXPERT_SLOTS)
        hpre = pre_activation(c, slot)
        if hpre_prev is not None:
            combine(c - 1, (slot - 1) % EXPERT_SLOTS, hpre_prev)
        return hpre

    hpre = None
    for c in range(EXPERT_SLOTS):
        hpre = region(c, c, hpre)

    def group(g, hpre):
        for slot in range(EXPERT_SLOTS):
            hpre = region(g * EXPERT_SLOTS + slot, slot, hpre)
        return hpre

    hpre = lax.fori_loop(1, tb // EXPERT_SLOTS, group, hpre)
    combine(tb - 1, (tb - 1) % EXPERT_SLOTS, hpre)
    for slot in range(EXPERT_AHEAD):
        wait(slot)


def _peer_experts(eidx, xn, gate, h, table):
    t, d = h.shape
    n_sel = eidx.shape[1]
    ns = d // LANES
    tb = EXPERT_TOKEN_BLOCK
    assert t % tb == 0 and tb % EXPERT_SLOTS == 0 and tb >= 2 * EXPERT_SLOTS
    row3 = pl.BlockSpec((tb, ns, LANES), lambda i: (i, 0, 0))
    out = pl.pallas_call(
        _peer_experts_kernel,
        grid=(t // tb,),
        in_specs=[pl.BlockSpec(memory_space=pl.ANY), row3, pl.BlockSpec((tb, n_sel), lambda i: (i, 0)), row3,
                  pl.BlockSpec(memory_space=pl.ANY)],
        out_specs=row3,
        out_shape=jax.ShapeDtypeStruct((t, ns, LANES), F32),
        scratch_shapes=([pltpu.SMEM((2 * tb, n_sel), I32)] + [pltpu.VMEM((ns, n_sel, LANES), I32)] * EXPERT_SLOTS
                        + [pltpu.SemaphoreType.DMA((2,)), pltpu.SemaphoreType.DMA((EXPERT_SLOTS,))]),
        compiler_params=_cparams(("arbitrary",)),
    )(eidx, xn.reshape(t, ns, LANES), gate, h.reshape(t, ns, LANES), table.reshape(-1, LANES))
    return out.reshape(t, d)


def _ple_kernel(h_ref, g_ref, wg_ref, p_ref, wp_ref, o_ref):
    h = h_ref[...]
    gate = jax.nn.sigmoid(jnp.dot(_rmsnorm_rows(h, g_ref[...]).astype(BF16), wg_ref[...],
                                  preferred_element_type=F32))
    o_ref[...] = h + gate * jnp.dot(p_ref[...].astype(BF16), wp_ref[...], preferred_element_type=F32)


def _final_norm_kernel(h_ref, g_ref, o_ref):
    o_ref[...] = _rmsnorm_rows(h_ref[...], g_ref[...])


def _ple(h, gain, w_gate, p, w_proj):
    t, d = h.shape
    tm = _token_tile(t)
    row = lambda w: pl.BlockSpec((tm, w), lambda i: (i, 0))
    return pl.pallas_call(
        _ple_kernel,
        grid=(t // tm,),
        in_specs=[row(d), _resident((1, d)), _resident(w_gate.shape), row(p.shape[1]), _resident(w_proj.shape)],
        out_specs=row(d),
        out_shape=jax.ShapeDtypeStruct((t, d), F32),
        compiler_params=_cparams(("parallel",)),
    )(h, gain, w_gate, p, w_proj)


def _final_norm(h, gain):
    t, d = h.shape
    tm = _token_tile(t)
    row = pl.BlockSpec((tm, d), lambda i: (i, 0))
    return pl.pallas_call(
        _final_norm_kernel,
        grid=(t // tm,),
        in_specs=[row, _resident((1, d))],
        out_specs=row,
        out_shape=jax.ShapeDtypeStruct((t, d), F32),
        compiler_params=_cparams(("parallel",)),
    )(h, gain)


def _rope_tables(pos, head_width):
    rot = head_width // ROT_FRACTION
    half = rot // 2
    inv = ROPE_THETA ** (-jnp.arange(half, dtype=F32) * 2.0 / rot)
    ang = pos.astype(F32)[:, None] * inv[None, :]
    cos, sin = jnp.cos(ang), jnp.sin(ang)
    n = pos.shape[0]
    zeros = lambda w: jnp.zeros((n, w), F32)
    c = jnp.concatenate([cos, cos, jnp.ones((n, head_width - rot), F32)], axis=1)
    m = jnp.concatenate([-sin, zeros(head_width - half)], axis=1)
    p = jnp.concatenate([zeros(half), sin, zeros(head_width - rot)], axis=1)
    reps = LANES // head_width
    return tuple(jnp.tile(a, (1, reps)) for a in (c, m, p))


def _pack_w_in(w_in, conv_ch):
    d = w_in.shape[0]
    o_k = ATTN_WIDTH
    o_v = o_k + KV_WIDTH
    o_qi = o_v + KV_WIDTH
    o_ki = o_qi + QI_WIDTH
    o_wi = o_ki + IDX_DIM
    o_glu = o_wi + IDX_HEADS
    z = lambda w: jnp.zeros((d, w), w_in.dtype)
    return jnp.concatenate([
        w_in[:, :o_ki], w_in[:, o_ki:o_wi], z(LANES - IDX_DIM), w_in[:, o_wi:o_glu], z(LANES - IDX_HEADS),
        w_in[:, o_glu:o_glu + 2 * conv_ch]], axis=1).astype(BF16)


def _pack_experts(u, v):
    pair = jnp.stack([v.astype(jnp.bfloat16), u.astype(jnp.bfloat16)], axis=-1)
    return lax.bitcast_convert_type(pair, I32)


def kernel(x_prompt, x_sample, cache_k, cache_v, cache_kidx, state_conv, page_table, p_prompt, p_sample, attn_norm, w_in, conv_w, conv_b, conv_ln_g, conv_ln_b, w_out, ffn_norm, peer_wq, peer_subkeys, peer_u, peer_v, ple_norm, ple_gate, ple_proj, final_norm):
    batch, seq, d = x_prompt.shape
    db, dec_seq, _ = x_sample.shape
    depth = attn_norm.shape[0]
    n_pool, page = cache_k.shape[1], cache_k.shape[2]
    n_pages = page_table.shape[1]
    past = n_pages * page
    conv_ch = d - ATTN_WIDTH
    tp = batch * seq
    assert seq % LANES == 0 and tp % LANES == 0 and (tp + db * dec_seq) % LANES == 0
    assert page == LANES and dec_seq <= DEC_SEQ_PAD and db % 8 == 0

    pos = jnp.concatenate([jnp.tile(jnp.arange(seq), batch), jnp.tile(past + jnp.arange(dec_seq), db)])
    tables = _rope_tables(pos, HEAD_DIM) + _rope_tables(pos, IDX_DIM)

    h = jnp.concatenate([x_prompt.reshape(tp, d), x_sample.reshape(db * dec_seq, d)], axis=0)
    outs = [[] for _ in range(8)]
    for i in range(depth):
        w_p = _pack_w_in(w_in[i], conv_ch)
        q, k, v, qi, ki, wi, glu, kb, vb, kib = _inproj(h, attn_norm[i][None], w_p, tables, conv_ch)

        attn_p = _attn_prompt(q[:tp], qi[:tp], wi[:tp], kb[:tp], vb[:tp], kib[:tp], batch, seq)
        attn_s = _attn_sample(q[tp:], qi[tp:], wi[tp:], k[tp:], v[tp:], ki[tp:],
                              cache_k, cache_v, cache_kidx, i, page_table, dec_seq)
        row = lambda a: a[None]
        conv_p, st_p = _conv_prompt(glu[:tp], conv_w[i], row(conv_b[i]), row(conv_ln_g[i]), row(conv_ln_b[i]),
                                    batch, seq)
        conv_s, st_s = _conv_sample(glu[tp:], state_conv[i], conv_w[i], row(conv_b[i]), row(conv_ln_g[i]),
                                    row(conv_ln_b[i]), dec_seq)
        w_o = w_out[i].astype(BF16)
        h = _outproj(h, jnp.concatenate([attn_p, attn_s], axis=0), jnp.concatenate([conv_p, conv_s], axis=0),
                     w_o[:ATTN_WIDTH], w_o[ATTN_WIDTH:])

        sk = peer_subkeys[i].reshape(PEER_HEADS * 2, PEER_KEYS, -1).astype(BF16)
        xn, eidx, gate = _peer_select(h, row(ffn_norm[i]), peer_wq[i].T.astype(BF16), sk)
        h = _peer_experts(eidx, xn, gate, h, _pack_experts(peer_u[i], peer_v[i]))

        p_all = jnp.concatenate([p_prompt[i].reshape(tp, -1), p_sample[i].reshape(db * dec_seq, -1)], axis=0)
        h = _ple(h, row(ple_norm[i]), ple_gate[i].astype(BF16), p_all, ple_proj[i].astype(BF16))

        outs[0].append(k[:tp].reshape(batch, seq, N_KV_HEADS, HEAD_DIM))
        outs[1].append(v[:tp].reshape(batch, seq, N_KV_HEADS, HEAD_DIM))
        outs[2].append(ki[:tp].reshape(batch, seq, IDX_DIM))
        outs[3].append(st_p)
        outs[4].append(k[tp:].reshape(db, dec_seq, N_KV_HEADS, HEAD_DIM))
        outs[5].append(v[tp:].reshape(db, dec_seq, N_KV_HEADS, HEAD_DIM))
        outs[6].append(ki[tp:].reshape(db, dec_seq, IDX_DIM))
        outs[7].append(st_s)

    y = _final_norm(h, final_norm[None])
    return (y[:tp].reshape(batch, seq, d), y[tp:].reshape(db, dec_seq, d)) + tuple(jnp.stack(o) for o in outs)
```

```python
import functools

import numpy as np
import jax
import jax.numpy as jnp
from jax import lax
from jax.experimental import pallas as pl
from jax.experimental.pallas import tpu as pltpu

N_HEADS = 8
HEAD_DIM = 128
N_KV_HEADS = 2
IDX_HEADS = 8
IDX_DIM = 64
INDEX_TOPK = 256
CONV_WIDTH = 31
ROPE_THETA = 500000.0
ROT_FRACTION = 4
PEER_HEADS = 8
PEER_KEYS = 128
PEER_TOPK = 16
EPS = 1e-6
DEC_SEQ_PAD = 8

LANES = 128
VMEM_LIMIT = 56 * 1024 * 1024

F32 = jnp.float32
BF16 = jnp.bfloat16
I32 = jnp.int32
INT_MIN = np.int32(-2 ** 31)
NT_DIMS = (((1,), (1,)), ((), ()))

ATTN_WIDTH = N_HEADS * HEAD_DIM
KV_WIDTH = N_KV_HEADS * HEAD_DIM
QI_WIDTH = IDX_HEADS * IDX_DIM
P_Q = 0
P_K = P_Q + ATTN_WIDTH
P_V = P_K + KV_WIDTH
P_QI = P_V + KV_WIDTH
P_KI = P_QI + QI_WIDTH
P_WI = P_KI + LANES
P_GLU = P_WI + LANES


def _cparams(sem=None):
    return pltpu.CompilerParams(dimension_semantics=sem, vmem_limit_bytes=VMEM_LIMIT)


def _resident(shape):
    nd = len(shape)
    return pl.BlockSpec(shape, lambda *_: (0,) * nd, pipeline_mode=pl.Buffered(1))


def _rmsnorm_rows(x, g):
    ms = jnp.mean(x * x, axis=-1, keepdims=True)
    return x * lax.rsqrt(ms + EPS) * g


def _code_to_float(code):
    s = code ^ INT_MIN
    return pltpu.bitcast(s ^ ((s >> 31) & np.int32(0x7FFFFFFF)), F32)


def _kth_largest(count_ge, k, shape):
    def step(i, code):
        cand = code | (jnp.int32(1) << (31 - i))
        return jnp.where(count_ge(_code_to_float(cand)) >= k, cand, code)

    code = lax.fori_loop(0, 32, step, jnp.zeros(shape, I32), unroll=2)
    return jnp.where(code == 0, -jnp.inf, _code_to_float(code))


def _token_tile(t):
    return 256 if t % 256 == 0 else 128


def _inproj_kernel(x_ref, g_ref, w_ref, c128, m128, p128, c64, m64, p64,
                   q_ref, k_ref, v_ref, qi_ref, ki_ref, wi_ref, glu_ref, kb_ref, vb_ref, kib_ref,
                   *, conv_ch):
    xn = _rmsnorm_rows(x_ref[...], g_ref[...]).astype(BF16)

    def mm(lo, hi):
        return jnp.dot(xn, w_ref[:, lo:hi], preferred_element_type=F32)

    def rope(z, c, m, p, half):
        return z * c + pltpu.roll(z, LANES - half, 1) * m + pltpu.roll(z, half, 1) * p

    rot_a = HEAD_DIM // ROT_FRACTION // 2
    rot_b = IDX_DIM // ROT_FRACTION // 2
    ca, ma, pa = c128[...], m128[...], p128[...]
    cb, mb, pb = c64[...], m64[...], p64[...]

    zq = mm(P_Q, P_K)
    for h in range(N_HEADS):
        sl = slice(h * HEAD_DIM, (h + 1) * HEAD_DIM)
        q_ref[:, sl] = (rope(zq[:, sl], ca, ma, pa, rot_a) * (HEAD_DIM ** -0.5)).astype(BF16)
    zk = mm(P_K, P_V)
    for h in range(N_KV_HEADS):
        sl = slice(h * HEAD_DIM, (h + 1) * HEAD_DIM)
        r = rope(zk[:, sl], ca, ma, pa, rot_a)
        k_ref[:, sl] = r
        kb_ref[:, sl] = r.astype(BF16)
    zv = mm(P_V, P_QI)
    v_ref[...] = zv
    vb_ref[...] = zv.astype(BF16)
    zqi = mm(P_QI, P_KI)
    for s in range(QI_WIDTH // LANES):
        sl = slice(s * LANES, (s + 1) * LANES)
        qi_ref[:, sl] = (rope(zqi[:, sl], cb, mb, pb, rot_b) * (IDX_DIM ** -0.5)).astype(BF16)
    r = rope(mm(P_KI, P_WI), cb, mb, pb, rot_b)[:, :IDX_DIM]
    ki_ref[...] = r
    kib_ref[...] = r.astype(BF16)
    wi_ref[...] = mm(P_WI, P_GLU) * (IDX_HEADS ** -0.5)
    za = mm(P_GLU, P_GLU + conv_ch)
    zb = mm(P_GLU + conv_ch, P_GLU + 2 * conv_ch)
    glu_ref[...] = za * jax.nn.sigmoid(zb)


def _inproj(x, gain, w_p, tables, conv_ch):
    t, d = x.shape
    tm = _token_tile(t)
    npad = w_p.shape[1]
    row = lambda w: pl.BlockSpec((tm, w), lambda i: (i, 0))
    out_shapes = [
        jax.ShapeDtypeStruct((t, ATTN_WIDTH), BF16),
        jax.ShapeDtypeStruct((t, KV_WIDTH), F32),
        jax.ShapeDtypeStruct((t, KV_WIDTH), F32),
        jax.ShapeDtypeStruct((t, QI_WIDTH), BF16),
        jax.ShapeDtypeStruct((t, IDX_DIM), F32),
        jax.ShapeDtypeStruct((t, LANES), F32),
        jax.ShapeDtypeStruct((t, conv_ch), F32),
        jax.ShapeDtypeStruct((t, KV_WIDTH), BF16),
        jax.ShapeDtypeStruct((t, KV_WIDTH), BF16),
        jax.ShapeDtypeStruct((t, IDX_DIM), BF16),
    ]
    return pl.pallas_call(
        functools.partial(_inproj_kernel, conv_ch=conv_ch),
        grid=(t // tm,),
        in_specs=[row(d), _resident((1, d)), _resident((d, npad))] + [row(LANES)] * 6,
        out_specs=[row(s.shape[1]) for s in out_shapes],
        out_shape=out_shapes,
        compiler_params=_cparams(("parallel",)),
    )(x, gain, w_p, *tables)


def _attn_prompt_tile(q_ref, qi_ref, wi_ref, k_ref, v_ref, ki_ref, ones_ref, tri_ref, o_ref, *, top_k, span):
    tq = q_ref.shape[0]
    n_chunks = span // LANES
    j = pl.program_id(1)
    qpos = j * tq + lax.broadcasted_iota(I32, (tq, LANES), 0)
    lane = lax.broadcasted_iota(I32, (tq, LANES), 1)

    ki = ki_ref[0:span, :]
    wi = wi_ref[...]
    score = jnp.zeros((tq, span), F32)
    for h in range(IDX_HEADS):
        s = lax.dot_general(qi_ref[:, h * IDX_DIM:(h + 1) * IDX_DIM], ki, NT_DIMS,
                            preferred_element_type=F32)
        score = score + jnp.maximum(s, 0.0) * wi[:, h:h + 1]
    causal = [(c * LANES + lane) <= qpos for c in range(n_chunks)]
    keys = [jnp.where(causal[c], score[:, c * LANES:(c + 1) * LANES], -jnp.inf) for c in range(n_chunks)]

    ones = ones_ref[...]

    def lane_total(x):
        return jnp.dot(x.astype(BF16), ones, preferred_element_type=F32)

    def count(cmp):
        cnt = jnp.zeros((tq, LANES), F32)
        for c in range(n_chunks):
            cnt = cnt + jnp.where(cmp(keys[c]), 1.0, 0.0)
        return lane_total(cnt)

    kf = float(top_k)
    thr = _kth_largest(lambda t: count(lambda kc: kc >= t), kf, (tq, LANES))

    need = kf - count(lambda kc: kc > thr)
    tri = tri_ref[...]
    offs = jnp.zeros((tq, LANES), F32)
    bias_chunks = []
    for c in range(n_chunks):
        eq_c = keys[c] == thr
        eq_b = jnp.where(eq_c, 1.0, 0.0).astype(BF16)
        prefix = jnp.dot(eq_b, tri, preferred_element_type=F32) + offs
        sel = (keys[c] > thr) | (eq_c & (prefix <= need))
        bias_chunks.append(jnp.where(sel & causal[c], 0.0, -jnp.inf))
        offs = offs + jnp.dot(eq_b, ones, preferred_element_type=F32)
    bias = jnp.concatenate(bias_chunks, axis=1)

    heads_per_group = N_HEADS // N_KV_HEADS
    for g in range(N_KV_HEADS):
        kg = k_ref[0:span, g * HEAD_DIM:(g + 1) * HEAD_DIM]
        vg = v_ref[0:span, g * HEAD_DIM:(g + 1) * HEAD_DIM]
        for r in range(heads_per_group):
            sl = slice((g * heads_per_group + r) * HEAD_DIM, (g * heads_per_group + r + 1) * HEAD_DIM)
            logits = lax.dot_general(q_ref[:, sl], kg, NT_DIMS, preferred_element_type=F32) + bias
            m = jnp.max(logits, axis=-1, keepdims=True)
            p = jnp.exp(logits - m)
            den = jnp.sum(p, axis=-1, keepdims=True)
            o = jnp.dot(p.astype(BF16), vg, preferred_element_type=F32)
            o_ref[:, sl] = (o / den).astype(o_ref.dtype)


def _attn_prompt_kernel(*refs, top_k, n_cls):
    tq = refs[0].shape[0]
    nq = refs[3].shape[0] // tq
    width = nq // n_cls
    j = pl.program_id(1)
    for cls in range(n_cls):
        @pl.when(j // width == cls)
        def _(cls=cls):
            _attn_prompt_tile(*refs, top_k=top_k, span=(cls + 1) * width * tq)


def _attn_prompt(q, qi, wi, kb, vb, kib, batch, seq):
    tq = LANES
    top_k = min(INDEX_TOPK, seq // 4)
    ones = jnp.ones((LANES, LANES), BF16)
    tri = jnp.asarray(np.arange(LANES)[:, None] <= np.arange(LANES)[None, :], BF16)
    nq = seq // tq
    n_cls = 4 if nq % 4 == 0 else (2 if nq % 2 == 0 else 1)
    qrow = lambda w: pl.BlockSpec((tq, w), lambda b, j: (b * nq + j, 0))
    kvrow = lambda w: pl.BlockSpec((seq, w), lambda b, j: (b, 0))
    return pl.pallas_call(
        functools.partial(_attn_prompt_kernel, top_k=top_k, n_cls=n_cls),
        grid=(batch, nq),
        in_specs=[qrow(ATTN_WIDTH), qrow(QI_WIDTH), qrow(LANES), kvrow(KV_WIDTH), kvrow(KV_WIDTH), kvrow(IDX_DIM),
                  _resident((LANES, LANES)), _resident((LANES, LANES))],
        out_specs=qrow(ATTN_WIDTH),
        out_shape=jax.ShapeDtypeStruct((batch * seq, ATTN_WIDTH), BF16),
        compiler_params=_cparams(("parallel", "parallel")),
    )(q, qi, wi, kb, vb, kib, ones, tri)


def _attn_sample_kernel(pt_ref, qg_ref, qi_ref, wcol_ref, knew_ref, vnew_ref, kinew_ref, *rest,
                        n_pages, top_k, past, dec_seq):
    del pt_ref
    k_pages = rest[:n_pages]
    v_pages = rest[n_pages:2 * n_pages]
    ki_pages = rest[2 * n_pages:3 * n_pages]
    tri_ref, ones_ref, o_ref, kn_s, vn_s, kin_s = rest[3 * n_pages:]
    page = k_pages[0].shape[0]
    n_chunks = n_pages + 1
    rows = DEC_SEQ_PAD

    for scr, new in ((kn_s, knew_ref), (vn_s, vnew_ref), (kin_s, kinew_ref)):
        scr[...] = jnp.zeros(scr.shape, scr.dtype)
        scr[0:dec_seq, :] = new[...]

    def chunk(pages, scr, c, g):
        if c < n_pages:
            return pages[c][:, g, :].astype(BF16)
        return scr[:, g * HEAD_DIM:(g + 1) * HEAD_DIM].astype(BF16)

    def ki_chunk(c):
        return (ki_pages[c] if c < n_pages else kin_s)[...].astype(BF16)

    row = lax.broadcasted_iota(I32, (rows, page), 0)
    lane = lax.broadcasted_iota(I32, (rows, page), 1)
    qpos = past + row

    qi = qi_ref[...]
    wcol = wcol_ref[...]
    keys, causals = [], []
    for c in range(n_chunks):
        s = lax.dot_general(qi, ki_chunk(c), NT_DIMS, preferred_element_type=F32)
        s = jnp.maximum(s, 0.0) * wcol
        score = s[0:rows]
        for h in range(1, IDX_HEADS):
            score = score + s[h * rows:(h + 1) * rows]
        causal_c = (c * page + lane) <= qpos
        causals.append(causal_c)
        keys.append(jnp.where(causal_c, score, -jnp.inf))

    def count(cmp):
        tot = jnp.zeros((rows, page), F32)
        for c in range(n_chunks):
            tot = tot + jnp.where(cmp(keys[c]), 1.0, 0.0)
        return jnp.broadcast_to(jnp.sum(tot, axis=1, keepdims=True), (rows, page))

    kf = float(top_k)
    thr = _kth_largest(lambda t: count(lambda kc: kc >= t), kf, (rows, page))
    need = kf - count(lambda kc: kc > thr)

    tri = tri_ref[...]
    ones = ones_ref[...]
    offs = jnp.zeros((rows, page), F32)
    biases = []
    for c in range(n_chunks):
        eq_c = keys[c] == thr
        eq_f = jnp.where(eq_c, 1.0, 0.0)
        prefix = jnp.dot(eq_f, tri, preferred_element_type=F32) + offs
        sel = (keys[c] > thr) | (eq_c & (prefix <= need))
        b8 = jnp.where(sel & causals[c], 0.0, -jnp.inf)
        biases.append(jnp.concatenate([b8] * (N_HEADS // N_KV_HEADS), axis=0))
        offs = offs + jnp.dot(eq_f, ones, preferred_element_type=F32)

    heads_per_group = N_HEADS // N_KV_HEADS
    for g in range(N_KV_HEADS):
        qg = qg_ref[g]
        logits = [lax.dot_general(qg, chunk(k_pages, kn_s, c, g), NT_DIMS, preferred_element_type=F32)
                  + biases[c] for c in range(n_chunks)]
        m = logits[0]
        for c in range(1, n_chunks):
            m = jnp.maximum(m, logits[c])
        m = jnp.max(m, axis=1, keepdims=True)
        den = jnp.zeros(logits[0].shape, F32)
        o = jnp.zeros((heads_per_group * rows, HEAD_DIM), F32)
        for c in range(n_chunks):
            p = jnp.exp(logits[c] - m)
            den = den + p
            o = o + jnp.dot(p.astype(BF16), chunk(v_pages, vn_s, c, g), preferred_element_type=F32)
        o = o / jnp.sum(den, axis=1, keepdims=True)
        for r in range(heads_per_group):
            o_ref[g * heads_per_group + r] = o[r * rows:(r + 1) * rows, :]


def _attn_sample(q_s, qi_s, wi_s, k_s, v_s, ki_s, cache_k, cache_v, cache_ki, layer, page_table, dec_seq):
    db, n_pages = page_table.shape
    page = cache_k.shape[2]
    past = n_pages * page
    top_k = min(INDEX_TOPK, (past + dec_seq) // 4)
    hpg = N_HEADS // N_KV_HEADS
    pad = DEC_SEQ_PAD - dec_seq
    qg = q_s.reshape(db, dec_seq, N_KV_HEADS, hpg, HEAD_DIM).transpose(0, 2, 3, 1, 4)
    qg = jnp.pad(qg, ((0, 0), (0, 0), (0, 0), (0, pad), (0, 0))).reshape(db, N_KV_HEADS, hpg * DEC_SEQ_PAD, HEAD_DIM)
    qi = qi_s.reshape(db, dec_seq, IDX_HEADS, IDX_DIM).transpose(0, 2, 1, 3)
    qi = jnp.pad(qi, ((0, 0), (0, 0), (0, pad), (0, 0))).reshape(db, IDX_HEADS * DEC_SEQ_PAD, IDX_DIM)
    wcol = wi_s[:, :IDX_HEADS].reshape(db, dec_seq, IDX_HEADS).transpose(0, 2, 1)
    wcol = jnp.pad(wcol, ((0, 0), (0, 0), (0, pad))).reshape(db, IDX_HEADS * DEC_SEQ_PAD, 1)
    wcol = jnp.broadcast_to(wcol, (db, IDX_HEADS * DEC_SEQ_PAD, page))
    knew = k_s.reshape(db, dec_seq, KV_WIDTH)
    vnew = v_s.reshape(db, dec_seq, KV_WIDTH)
    kinew = ki_s.reshape(db, dec_seq, IDX_DIM)
    tri = jnp.asarray(np.arange(page)[:, None] <= np.arange(page)[None, :], F32)
    ones = jnp.ones((page, page), F32)

    def seq_spec(shape):
        nd = len(shape)
        return pl.BlockSpec((None,) + tuple(shape[1:]), lambda b, pt: (b,) + (0,) * (nd - 1))

    def page_spec(tail, p):
        return pl.BlockSpec((None, None, page) + tail,
                            lambda b, pt: (layer, pt[b, p], 0) + (0,) * len(tail))

    kv_specs = [page_spec((N_KV_HEADS, HEAD_DIM), p) for p in range(n_pages)]

    const = lambda shape: pl.BlockSpec(shape, lambda b, pt: (0,) * len(shape))
    in_specs = [seq_spec(a.shape) for a in (qg, qi, wcol, knew, vnew, kinew)]
    in_specs += kv_specs + kv_specs + [page_spec((IDX_DIM,), p) for p in range(n_pages)]
    in_specs += [const((page, page)), const((page, page))]
    out = pl.pallas_call(
        functools.partial(_attn_sample_kernel, n_pages=n_pages, top_k=top_k, past=past, dec_seq=dec_seq),
        grid_spec=pltpu.PrefetchScalarGridSpec(
            num_scalar_prefetch=1,
            grid=(db,),
            in_specs=in_specs,
            out_specs=pl.BlockSpec((None, N_HEADS, DEC_SEQ_PAD, HEAD_DIM), lambda b, pt: (b, 0, 0, 0)),
            scratch_shapes=[pltpu.VMEM((page, KV_WIDTH), F32), pltpu.VMEM((page, KV_WIDTH), F32),
                            pltpu.VMEM((page, IDX_DIM), F32)],
        ),
        out_shape=jax.ShapeDtypeStruct((db, N_HEADS, DEC_SEQ_PAD, HEAD_DIM), F32),
        compiler_params=_cparams(("arbitrary",)),
    )(page_table, qg, qi, wcol, knew, vnew, kinew,
      *([cache_k] * len(kv_specs)), *([cache_v] * len(kv_specs)), *([cache_ki] * n_pages), tri, ones)
    out = out[:, :, :dec_seq, :].transpose(0, 2, 1, 3)
    return out.reshape(db * dec_seq, ATTN_WIDTH).astype(BF16)


def _ln_swish(y, g, b):
    mu = jnp.mean(y, axis=-1, keepdims=True)
    yc = y - mu
    var = jnp.mean(yc * yc, axis=-1, keepdims=True)
    yn = yc * lax.rsqrt(var + EPS) * g + b
    return yn * jax.nn.sigmoid(yn)


CONV_HALO = 32


def _conv_prompt_kernel(glu_ref, w_ref, cb_ref, g_ref, b_ref, o_ref, st_ref, seq_s, y_s):
    tt, ch = glu_ref.shape
    t = pl.program_id(1)

    @pl.when(t == 0)
    def _():
        seq_s[0:CONV_HALO, :] = jnp.zeros((CONV_HALO, ch), F32)

    @pl.when(t > 0)
    def _():
        seq_s[0:CONV_HALO, :] = seq_s[tt:tt + CONV_HALO, :]

    seq_s[CONV_HALO:CONV_HALO + tt, :] = glu_ref[...]

    rc, cc = 32, 256
    first = CONV_HALO - (CONV_WIDTH - 1)
    for c0 in range(0, ch, cc):
        for r0 in range(0, tt, rc):
            acc = jnp.zeros((rc, cc), F32)
            for j in range(CONV_WIDTH):
                lo = first + r0 + j
                acc = acc + seq_s[lo:lo + rc, c0:c0 + cc] * w_ref[j:j + 1, c0:c0 + cc]
            y_s[r0:r0 + rc, c0:c0 + cc] = acc
    o_ref[...] = _ln_swish(y_s[...] + cb_ref[...], g_ref[...], b_ref[...]).astype(o_ref.dtype)

    @pl.when(t == pl.num_programs(1) - 1)
    def _():
        st_ref[...] = seq_s[CONV_HALO + tt - (CONV_WIDTH - 1):CONV_HALO + tt, :]


def _conv_prompt(glu, conv_w, conv_b, ln_g, ln_b, batch, seq):
    ch = glu.shape[1]
    tt = LANES
    nt = seq // tt
    const = lambda shape: pl.BlockSpec(shape, lambda b, t: (0,) * len(shape))
    return pl.pallas_call(
        _conv_prompt_kernel,
        grid=(batch, nt),
        in_specs=[pl.BlockSpec((tt, ch), lambda b, t: (b * nt + t, 0)),
                  const((CONV_WIDTH, ch)), const((1, ch)), const((1, ch)), const((1, ch))],
        out_specs=[pl.BlockSpec((tt, ch), lambda b, t: (b * nt + t, 0)),
                   pl.BlockSpec((None, CONV_WIDTH - 1, ch), lambda b, t: (b, 0, 0))],
        out_shape=[jax.ShapeDtypeStruct((batch * seq, ch), BF16),
                   jax.ShapeDtypeStruct((batch, CONV_WIDTH - 1, ch), F32)],
        scratch_shapes=[pltpu.VMEM((CONV_HALO + tt, ch), F32), pltpu.VMEM((tt, ch), F32)],
        compiler_params=_cparams(("parallel", "arbitrary")),
    )(glu, conv_w, conv_b, ln_g, ln_b)


def _conv_sample_kernel(glu_ref, prev_ref, wsh_ref, w_ref, cb_ref, g_ref, b_ref, o_ref, st_ref, y_s, *, dec_seq):
    nb = prev_ref.shape[0]
    keep = CONV_WIDTH - 1 - dec_seq
    for b in range(nb):
        prev = prev_ref[b]
        for t in range(dec_seq):
            y = jnp.sum(prev * wsh_ref[t], axis=0, keepdims=True)
            for u in range(t + 1):
                j = CONV_WIDTH - 1 - t + u
                y = y + glu_ref[b * dec_seq + u:b * dec_seq + u + 1, :] * w_ref[j:j + 1, :]
            y_s[b * dec_seq + t:b * dec_seq + t + 1, :] = y
        st_ref[b, 0:keep, :] = prev_ref[b, dec_seq:CONV_WIDTH - 1, :]
        st_ref[b, keep:CONV_WIDTH - 1, :] = glu_ref[b * dec_seq:(b + 1) * dec_seq, :]
    o_ref[...] = _ln_swish(y_s[...] + cb_ref[...], g_ref[...], b_ref[...]).astype(o_ref.dtype)


def _conv_sample(glu, prev, conv_w, conv_b, ln_g, ln_b, dec_seq):
    db = prev.shape[0]
    ch = glu.shape[1]
    nb = 8
    wsh = jnp.stack([jnp.pad(conv_w[:CONV_WIDTH - 1 - t], ((t, 0), (0, 0))) for t in range(dec_seq)])
    const = lambda shape: pl.BlockSpec(shape, lambda i: (0,) * len(shape))
    return pl.pallas_call(
        functools.partial(_conv_sample_kernel, dec_seq=dec_seq),
        grid=(db // nb,),
        in_specs=[pl.BlockSpec((nb * dec_seq, ch), lambda i: (i, 0)),
                  pl.BlockSpec((nb, CONV_WIDTH - 1, ch), lambda i: (i, 0, 0)),
                  const((dec_seq, CONV_WIDTH - 1, ch)), const((CONV_WIDTH, ch)),
                  const((1, ch)), const((1, ch)), const((1, ch))],
        out_specs=[pl.BlockSpec((nb * dec_seq, ch), lambda i: (i, 0)),
                   pl.BlockSpec((nb, CONV_WIDTH - 1, ch), lambda i: (i, 0, 0))],
        out_shape=[jax.ShapeDtypeStruct((db * dec_seq, ch), BF16),
                   jax.ShapeDtypeStruct((db, CONV_WIDTH - 1, ch), F32)],
        scratch_shapes=[pltpu.VMEM((nb * dec_seq, ch), F32)],
        compiler_params=_cparams(("parallel",)),
    )(glu, prev, wsh, conv_w, conv_b, ln_g, ln_b)


def _outproj_kernel(h_ref, a_ref, c_ref, wa_ref, wc_ref, o_ref):
    o_ref[...] = (h_ref[...]
                  + jnp.dot(a_ref[...], wa_ref[...], preferred_element_type=F32)
                  + jnp.dot(c_ref[...], wc_ref[...], preferred_element_type=F32))


def _outproj(h, attn, conv, w_attn, w_conv):
    t, d = h.shape
    tm = _token_tile(t)
    row = lambda w: pl.BlockSpec((tm, w), lambda i: (i, 0))
    return pl.pallas_call(
        _outproj_kernel,
        grid=(t // tm,),
        in_specs=[row(d), row(attn.shape[1]), row(conv.shape[1]), _resident(w_attn.shape), _resident(w_conv.shape)],
        out_specs=row(d),
        out_shape=jax.ShapeDtypeStruct((t, d), F32),
        compiler_params=_cparams(("parallel",)),
    )(h, attn, conv, w_attn, w_conv)


def _stair_table():
    k = PEER_TOPK
    rows = [b for b in range(k)]
    for a in range(1, 8):
        rows += [a * k + b if (a + 1) * (b + 1) <= k else -1 for b in range(8)]
    rows += [a * k for a in range(8, k)]
    return np.asarray(rows, np.int32)


def _top_rows(s, k, ids):
    big = np.int32(2 ** 30)
    out_rows = lax.broadcasted_iota(I32, (k, s.shape[1]), 0)
    vals = jnp.zeros((k, s.shape[1]), F32)
    sel_ids = jnp.zeros((k, s.shape[1]), I32)
    for r in range(k):
        m = jnp.max(s, axis=0, keepdims=True)
        first = jnp.min(jnp.where(s == m, ids, big), axis=0, keepdims=True)
        vals = jnp.where(out_rows == r, m, vals)
        sel_ids = jnp.where(out_rows == r, first, sel_ids)
        s = jnp.where(ids == first, -jnp.inf, s)
    return vals, sel_ids


def _peer_select_kernel(h_ref, g_ref, wqt_ref, sk_ref, flat_ref, xn_ref, eidx_ref, gate_ref):
    c = h_ref.shape[0]
    k = PEER_TOPK
    xn = _rmsnorm_rows(h_ref[...], g_ref[...])
    xn_ref[...] = xn
    qt = lax.dot_general(wqt_ref[...], xn.astype(BF16), NT_DIMS, preferred_element_type=F32)
    key_ids = lax.broadcasted_iota(I32, (PEER_KEYS, c), 0)
    flat = flat_ref[...]
    valid = flat >= 0
    e_rows, g_rows = [], []
    for h in range(PEER_HEADS):
        tops = []
        for n in range(2):
            hn = h * 2 + n
            qhn = qt[hn * PEER_KEYS:(hn + 1) * PEER_KEYS, :].astype(BF16)
            s = jnp.dot(sk_ref[hn], qhn, preferred_element_type=F32)
            tops.append(_top_rows(s, k, key_ids))
        (s1, i1), (s2, i2) = tops
        cv = [s1[0:1] + s2] + [s1[a:a + 1] + s2[0:8] for a in range(1, 8)] + [s1[8:k] + s2[0:1]]
        ce = ([i1[0:1] * PEER_KEYS + i2] + [i1[a:a + 1] * PEER_KEYS + i2[0:8] for a in range(1, 8)]
              + [i1[8:k] * PEER_KEYS + i2[0:1]])
        cand_v = jnp.where(valid, jnp.concatenate(cv, axis=0), -jnp.inf)
        cand_e = jnp.concatenate(ce, axis=0)
        big = np.int32(2 ** 30)
        out_rows = lax.broadcasted_iota(I32, (k, c), 0)
        best = jnp.zeros((k, c), F32)
        best_e = jnp.zeros((k, c), I32)
        for r in range(k):
            m = jnp.max(cand_v, axis=0, keepdims=True)
            first = jnp.min(jnp.where(cand_v == m, flat, big), axis=0, keepdims=True)
            hit = flat == first
            e = jnp.max(jnp.where(hit, cand_e, -1), axis=0, keepdims=True)
            best = jnp.where(out_rows == r, m, best)
            best_e = jnp.where(out_rows == r, e, best_e)
            cand_v = jnp.where(hit, -jnp.inf, cand_v)
        ex = jnp.exp(best - best[0:1])
        g_rows.append(ex / jnp.sum(ex, axis=0, keepdims=True))
        e_rows.append(best_e)
    eidx_ref[...] = jnp.concatenate(e_rows, axis=0).T
    gate_ref[...] = jnp.concatenate(g_rows, axis=0).T


def _peer_select(h, gain, wq_t, subkeys):
    t, d = h.shape
    c = _token_tile(t)
    n_sel = PEER_HEADS * PEER_TOPK
    flat = jnp.asarray(np.broadcast_to(_stair_table()[:, None], (_stair_table().shape[0], c)))
    row = lambda w: pl.BlockSpec((c, w), lambda i: (i, 0))
    return pl.pallas_call(
        _peer_select_kernel,
        grid=(t // c,),
        in_specs=[row(d), _resident((1, d)), _resident(wq_t.shape), _resident(subkeys.shape), _resident(flat.shape)],
        out_specs=[row(d), row(n_sel), row(n_sel)],
        out_shape=[jax.ShapeDtypeStruct((t, d), F32), jax.ShapeDtypeStruct((t, n_sel), I32),
                   jax.ShapeDtypeStruct((t, n_sel), F32)],
        compiler_params=_cparams(("parallel",)),
    )(h, gain, wq_t, subkeys, flat)


EXPERT_AHEAD = 6
EXPERT_SLOTS = EXPERT_AHEAD + 2
EXPERT_TOKEN_BLOCK = 192


def _split_hi_lo(x):
    hi = x.astype(BF16)
    return jnp.concatenate([hi, (x - hi.astype(F32)).astype(BF16)], axis=0)


def _peer_experts_kernel(eidx_hbm, xn_ref, gate_ref, h_ref, tbl_hbm, o_ref, idx_s, *rest):
    bufs = rest[:EXPERT_SLOTS]
    idx_sem, sem = rest[EXPERT_SLOTS:]
    tb, ns, _ = xn_ref.shape
    n_sel = gate_ref.shape[1]
    i = pl.program_id(0)
    cur = i % 2

    def idx_copy(step, slot):
        return pltpu.make_async_copy(eidx_hbm.at[pl.ds(step * tb, tb), :],
                                     idx_s.at[pl.ds(pl.multiple_of(slot * tb, 8), tb), :], idx_sem.at[slot])

    @pl.when(i == 0)
    def _():
        idx_copy(0, 0).start()

    idx_copy(i, cur).wait()

    @pl.when(i + 1 < pl.num_programs(0))
    def _():
        idx_copy(i + 1, 1 - cur).start()

    def issue(c, slot):
        for j in range(n_sel):
            src = tbl_hbm.at[pl.ds(pl.multiple_of(idx_s[cur * tb + c, j] * ns, ns), ns), :]
            pltpu.make_async_copy(src, bufs[slot].at[:, j, :], sem.at[slot]).start(priority=j % 2)

    def wait(slot):
        pltpu.make_async_copy(bufs[slot], bufs[slot], sem.at[slot]).wait()

    for c in range(EXPERT_AHEAD):
        issue(c, c)
    diag = lax.broadcasted_iota(I32, (2 * ns, n_sel), 0) % ns
    row = lax.broadcasted_iota(I32, (ns, n_sel), 0)

    def words(slot):
        return bufs[slot][...].reshape(ns * n_sel, LANES)

    def pre_activation(c, slot):
        u = pltpu.bitcast(words(slot) & np.int32(-65536), F32).astype(BF16)
        y = lax.dot_general(_split_hi_lo(xn_ref[c]), u, NT_DIMS, preferred_element_type=F32)
        acc = jnp.zeros((2 * ns, n_sel), F32)
        for s in range(ns):
            acc = acc + jnp.where(diag == s, y[:, s * n_sel:(s + 1) * n_sel], 0.0)
        return jnp.sum(acc, axis=0, keepdims=True)

    def combine(c, slot, hpre):
        gelu = 0.5 * hpre * (1.0 + lax.erf(hpre * (2.0 ** -0.5)))
        act = jnp.broadcast_to(gelu * gate_ref[pl.ds(c, 1), :], (ns, n_sel))
        a_bd = jnp.concatenate([jnp.where(row == s, act, 0.0) for s in range(ns)], axis=1)
        v = pltpu.bitcast(words(slot) << 16, F32).astype(BF16)
        o = jnp.dot(_split_hi_lo(a_bd), v, preferred_element_type=F32)
        o_ref[c] = h_ref[c] + o[:ns] + o[ns:]

    def region(c, slot, hpre_prev):
        wait(slot)
        issue(jnp.minimum(c + EXPERT_AHEAD, tb - 1), (slot + EXPERT_AHEAD) % EXPERT_SLOTS)
        hpre = pre_activation(c, slot)
        if hpre_prev is not None:
            combine(c - 1, (slot - 1) % EXPERT_SLOTS, hpre_prev)
        return hpre

    hpre = None
    for c in range(EXPERT_SLOTS):
        hpre = region(c, c, hpre)

    def group(g, hpre):
        for slot in range(EXPERT_SLOTS):
            hpre = region(g * EXPERT_SLOTS + slot, slot, hpre)
        return hpre

    hpre = lax.fori_loop(1, tb // EXPERT_SLOTS, group, hpre)
    combine(tb - 1, (tb - 1) % EXPERT_SLOTS, hpre)
    for slot in range(EXPERT_AHEAD):
        wait(slot)


def _peer_experts(eidx, xn, gate, h, table):
    t, d = h.shape
    n_sel = eidx.shape[1]
    ns = d // LANES
    tb = EXPERT_TOKEN_BLOCK
    assert t % tb == 0 and tb % EXPERT_SLOTS == 0 and tb >= 2 * EXPERT_SLOTS
    row3 = pl.BlockSpec((tb, ns, LANES), lambda i: (i, 0, 0))
    out = pl.pallas_call(
        _peer_experts_kernel,
        grid=(t // tb,),
        in_specs=[pl.BlockSpec(memory_space=pl.ANY), row3, pl.BlockSpec((tb, n_sel), lambda i: (i, 0)), row3,
                  pl.BlockSpec(memory_space=pl.ANY)],
        out_specs=row3,
        out_shape=jax.ShapeDtypeStruct((t, ns, LANES), F32),
        scratch_shapes=([pltpu.SMEM((2 * tb, n_sel), I32)] + [pltpu.VMEM((ns, n_sel, LANES), I32)] * EXPERT_SLOTS
                        + [pltpu.SemaphoreType.DMA((2,)), pltpu.SemaphoreType.DMA((EXPERT_SLOTS,))]),
        compiler_params=_cparams(("arbitrary",)),
    )(eidx, xn.reshape(t, ns, LANES), gate, h.reshape(t, ns, LANES), table.reshape(-1, LANES))
    return out.reshape(t, d)


def _ple_kernel(h_ref, g_ref, wg_ref, p_ref, wp_ref, o_ref):
    h = h_ref[...]
    gate = jax.nn.sigmoid(jnp.dot(_rmsnorm_rows(h, g_ref[...]).astype(BF16), wg_ref[...],
                                  preferred_element_type=F32))
    o_ref[...] = h + gate * jnp.dot(p_ref[...].astype(BF16), wp_ref[...], preferred_element_type=F32)


def _final_norm_kernel(h_ref, g_ref, o_ref):
    o_ref[...] = _rmsnorm_rows(h_ref[...], g_ref[...])


def _ple(h, gain, w_gate, p, w_proj):
    t, d = h.shape
    tm = _token_tile(t)
    row = lambda w: pl.BlockSpec((tm, w), lambda i: (i, 0))
    return pl.pallas_call(
        _ple_kernel,
        grid=(t // tm,),
        in_specs=[row(d), _resident((1, d)), _resident(w_gate.shape), row(p.shape[1]), _resident(w_proj.shape)],
        out_specs=row(d),
        out_shape=jax.ShapeDtypeStruct((t, d), F32),
        compiler_params=_cparams(("parallel",)),
    )(h, gain, w_gate, p, w_proj)


def _final_norm(h, gain):
    t, d = h.shape
    tm = _token_tile(t)
    row = pl.BlockSpec((tm, d), lambda i: (i, 0))
    return pl.pallas_call(
        _final_norm_kernel,
        grid=(t // tm,),
        in_specs=[row, _resident((1, d))],
        out_specs=row,
        out_shape=jax.ShapeDtypeStruct((t, d), F32),
        compiler_params=_cparams(("parallel",)),
    )(h, gain)


def _rope_tables(pos, head_width):
    rot = head_width // ROT_FRACTION
    half = rot // 2
    inv = ROPE_THETA ** (-jnp.arange(half, dtype=F32) * 2.0 / rot)
    ang = pos.astype(F32)[:, None] * inv[None, :]
    cos, sin = jnp.cos(ang), jnp.sin(ang)
    n = pos.shape[0]
    zeros = lambda w: jnp.zeros((n, w), F32)
    c = jnp.concatenate([cos, cos, jnp.ones((n, head_width - rot), F32)], axis=1)
    m = jnp.concatenate([-sin, zeros(head_width - half)], axis=1)
    p = jnp.concatenate([zeros(half), sin, zeros(head_width - rot)], axis=1)
    reps = LANES // head_width
    return tuple(jnp.tile(a, (1, reps)) for a in (c, m, p))


def _pack_w_in(w_in, conv_ch):
    d = w_in.shape[0]
    o_k = ATTN_WIDTH
    o_v = o_k + KV_WIDTH
    o_qi = o_v + KV_WIDTH
    o_ki = o_qi + QI_WIDTH
    o_wi = o_ki + IDX_DIM
    o_glu = o_wi + IDX_HEADS
    z = lambda w: jnp.zeros((d, w), w_in.dtype)
    return jnp.concatenate([
        w_in[:, :o_ki], w_in[:, o_ki:o_wi], z(LANES - IDX_DIM), w_in[:, o_wi:o_glu], z(LANES - IDX_HEADS),
        w_in[:, o_glu:o_glu + 2 * conv_ch]], axis=1).astype(BF16)


def _pack_experts(u, v):
    n, d = u.shape
    shape = (n, d // LANES, LANES)
    pair = jnp.stack([v.reshape(shape).astype(jnp.bfloat16), u.reshape(shape).astype(jnp.bfloat16)], axis=-1)
    return lax.bitcast_convert_type(pair, I32).reshape(n * (d // LANES), LANES)


def kernel(x_prompt, x_sample, cache_k, cache_v, cache_kidx, state_conv, page_table, p_prompt, p_sample, attn_norm, w_in, conv_w, conv_b, conv_ln_g, conv_ln_b, w_out, ffn_norm, peer_wq, peer_subkeys, peer_u, peer_v, ple_norm, ple_gate, ple_proj, final_norm):
    batch, seq, d = x_prompt.shape
    db, dec_seq, _ = x_sample.shape
    depth = attn_norm.shape[0]
    n_pool, page = cache_k.shape[1], cache_k.shape[2]
    n_pages = page_table.shape[1]
    past = n_pages * page
    conv_ch = d - ATTN_WIDTH
    tp = batch * seq
    assert seq % LANES == 0 and tp % LANES == 0 and (tp + db * dec_seq) % LANES == 0
    assert page == LANES and dec_seq <= DEC_SEQ_PAD and db % 8 == 0

    pos = jnp.concatenate([jnp.tile(jnp.arange(seq), batch), jnp.tile(past + jnp.arange(dec_seq), db)])
    tables = _rope_tables(pos, HEAD_DIM) + _rope_tables(pos, IDX_DIM)

    h = jnp.concatenate([x_prompt.reshape(tp, d), x_sample.reshape(db * dec_seq, d)], axis=0)
    outs = [[] for _ in range(8)]
    for i in range(depth):
        w_p = _pack_w_in(w_in[i], conv_ch)
        q, k, v, qi, ki, wi, glu, kb, vb, kib = _inproj(h, attn_norm[i][None], w_p, tables, conv_ch)

        attn_p = _attn_prompt(q[:tp], qi[:tp], wi[:tp], kb[:tp], vb[:tp], kib[:tp], batch, seq)
        attn_s = _attn_sample(q[tp:], qi[tp:], wi[tp:], k[tp:], v[tp:], ki[tp:],
                              cache_k, cache_v, cache_kidx, i, page_table, dec_seq)
        row = lambda a: a[None]
        conv_p, st_p = _conv_prompt(glu[:tp], conv_w[i], row(conv_b[i]), row(conv_ln_g[i]), row(conv_ln_b[i]),
                                    batch, seq)
        conv_s, st_s = _conv_sample(glu[tp:], state_conv[i], conv_w[i], row(conv_b[i]), row(conv_ln_g[i]),
                                    row(conv_ln_b[i]), dec_seq)
        w_o = w_out[i].astype(BF16)
        h = _outproj(h, jnp.concatenate([attn_p, attn_s], axis=0), jnp.concatenate([conv_p, conv_s], axis=0),
                     w_o[:ATTN_WIDTH], w_o[ATTN_WIDTH:])

        sk = peer_subkeys[i].reshape(PEER_HEADS * 2, PEER_KEYS, -1).astype(BF16)
        xn, eidx, gate = _peer_select(h, row(ffn_norm[i]), peer_wq[i].T.astype(BF16), sk)
        h = _peer_experts(eidx, xn, gate, h, _pack_experts(peer_u[i], peer_v[i]))

        p_all = jnp.concatenate([p_prompt[i].reshape(tp, -1), p_sample[i].reshape(db * dec_seq, -1)], axis=0)
        h = _ple(h, row(ple_norm[i]), ple_gate[i].astype(BF16), p_all, ple_proj[i].astype(BF16))

        outs[0].append(k[:tp].reshape(batch, seq, N_KV_HEADS, HEAD_DIM))
        outs[1].append(v[:tp].reshape(batch, seq, N_KV_HEADS, HEAD_DIM))
        outs[2].append(ki[:tp].reshape(batch, seq, IDX_DIM))
        outs[3].append(st_p)
        outs[4].append(k[tp:].reshape(db, dec_seq, N_KV_HEADS, HEAD_DIM))
        outs[5].append(v[tp:].reshape(db, dec_seq, N_KV_HEADS, HEAD_DIM))
        outs[6].append(ki[tp:].reshape(db, dec_seq, IDX_DIM))
        outs[7].append(st_s)

    y = _final_norm(h, final_norm[None])
    return (y[:tp].reshape(batch, seq, d), y[tp:].reshape(db, dec_seq, d)) + tuple(jnp.stack(o) for o in outs)
```

```python
import functools

import numpy as np
import jax
import jax.numpy as jnp
from jax import lax
from jax.experimental import pallas as pl
from jax.experimental.pallas import tpu as pltpu

N_HEADS = 8
HEAD_DIM = 128
N_KV_HEADS = 2
IDX_HEADS = 8
IDX_DIM = 64
INDEX_TOPK = 256
CONV_WIDTH = 31
ROPE_THETA = 500000.0
ROT_FRACTION = 4
PEER_HEADS = 8
PEER_KEYS = 128
PEER_TOPK = 16
EPS = 1e-6
DEC_SEQ_PAD = 8

LANES = 128
VMEM_LIMIT = 56 * 1024 * 1024

F32 = jnp.float32
BF16 = jnp.bfloat16
I32 = jnp.int32
INT_MIN = np.int32(-2 ** 31)
NT_DIMS = (((1,), (1,)), ((), ()))

ATTN_WIDTH = N_HEADS * HEAD_DIM
KV_WIDTH = N_KV_HEADS * HEAD_DIM
QI_WIDTH = IDX_HEADS * IDX_DIM
P_Q = 0
P_K = P_Q + ATTN_WIDTH
P_V = P_K + KV_WIDTH
P_QI = P_V + KV_WIDTH
P_KI = P_QI + QI_WIDTH
P_WI = P_KI + LANES
P_GLU = P_WI + LANES


def _cparams(sem=None):
    return pltpu.CompilerParams(dimension_semantics=sem, vmem_limit_bytes=VMEM_LIMIT)


def _resident(shape):
    nd = len(shape)
    return pl.BlockSpec(shape, lambda *_: (0,) * nd, pipeline_mode=pl.Buffered(1))


def _rmsnorm_rows(x, g):
    ms = jnp.mean(x * x, axis=-1, keepdims=True)
    return x * lax.rsqrt(ms + EPS) * g


def _code_to_float(code):
    s = code ^ INT_MIN
    return pltpu.bitcast(s ^ ((s >> 31) & np.int32(0x7FFFFFFF)), F32)


def _kth_largest(count_ge, k, shape, groups=1):
    def step(i, codes):
        bit = jnp.int32(1) << (31 - i)
        cands = [code | bit for code in codes]
        counts = [count_ge(g, _code_to_float(cands[g])) for g in range(groups)]
        return tuple(jnp.where(counts[g] >= k, cands[g], codes[g]) for g in range(groups))

    codes = lax.fori_loop(0, 32, step, tuple(jnp.zeros(shape, I32) for _ in range(groups)), unroll=2)
    return [jnp.where(code == 0, -jnp.inf, _code_to_float(code)) for code in codes]


def _token_tile(t):
    return 256 if t % 256 == 0 else 128


def _inproj_kernel(x_ref, g_ref, w_ref, c128, m128, p128, c64, m64, p64,
                   q_ref, k_ref, v_ref, qi_ref, ki_ref, wi_ref, glu_ref, kb_ref, vb_ref, kib_ref,
                   *, conv_ch):
    xn = _rmsnorm_rows(x_ref[...], g_ref[...]).astype(BF16)

    def mm(lo, hi):
        return jnp.dot(xn, w_ref[:, lo:hi], preferred_element_type=F32)

    def rope(z, c, m, p, half):
        return z * c + pltpu.roll(z, LANES - half, 1) * m + pltpu.roll(z, half, 1) * p

    rot_a = HEAD_DIM // ROT_FRACTION // 2
    rot_b = IDX_DIM // ROT_FRACTION // 2
    ca, ma, pa = c128[...], m128[...], p128[...]
    cb, mb, pb = c64[...], m64[...], p64[...]

    zq = mm(P_Q, P_K)
    for h in range(N_HEADS):
        sl = slice(h * HEAD_DIM, (h + 1) * HEAD_DIM)
        q_ref[:, sl] = (rope(zq[:, sl], ca, ma, pa, rot_a) * (HEAD_DIM ** -0.5)).astype(BF16)
    zk = mm(P_K, P_V)
    for h in range(N_KV_HEADS):
        sl = slice(h * HEAD_DIM, (h + 1) * HEAD_DIM)
        r = rope(zk[:, sl], ca, ma, pa, rot_a)
        k_ref[:, sl] = r
        kb_ref[:, sl] = r.astype(BF16)
    zv = mm(P_V, P_QI)
    v_ref[...] = zv
    vb_ref[...] = zv.astype(BF16)
    zqi = mm(P_QI, P_KI)
    for s in range(QI_WIDTH // LANES):
        sl = slice(s * LANES, (s + 1) * LANES)
        qi_ref[:, sl] = (rope(zqi[:, sl], cb, mb, pb, rot_b) * (IDX_DIM ** -0.5)).astype(BF16)
    r = rope(mm(P_KI, P_WI), cb, mb, pb, rot_b)[:, :IDX_DIM]
    ki_ref[...] = r
    kib_ref[...] = r.astype(BF16)
    wi_ref[...] = mm(P_WI, P_GLU) * (IDX_HEADS ** -0.5)
    za = mm(P_GLU, P_GLU + conv_ch)
    zb = mm(P_GLU + conv_ch, P_GLU + 2 * conv_ch)
    glu_ref[...] = za * jax.nn.sigmoid(zb)


def _inproj(x, gain, w_p, tables, conv_ch):
    t, d = x.shape
    tm = _token_tile(t)
    npad = w_p.shape[1]
    row = lambda w: pl.BlockSpec((tm, w), lambda i: (i, 0))
    out_shapes = [
        jax.ShapeDtypeStruct((t, ATTN_WIDTH), BF16),
        jax.ShapeDtypeStruct((t, KV_WIDTH), F32),
        jax.ShapeDtypeStruct((t, KV_WIDTH), F32),
        jax.ShapeDtypeStruct((t, QI_WIDTH), BF16),
        jax.ShapeDtypeStruct((t, IDX_DIM), F32),
        jax.ShapeDtypeStruct((t, LANES), F32),
        jax.ShapeDtypeStruct((t, conv_ch), F32),
        jax.ShapeDtypeStruct((t, KV_WIDTH), BF16),
        jax.ShapeDtypeStruct((t, KV_WIDTH), BF16),
        jax.ShapeDtypeStruct((t, IDX_DIM), BF16),
    ]
    return pl.pallas_call(
        functools.partial(_inproj_kernel, conv_ch=conv_ch),
        grid=(t // tm,),
        in_specs=[row(d), _resident((1, d)), _resident((d, npad))] + [row(LANES)] * 6,
        out_specs=[row(s.shape[1]) for s in out_shapes],
        out_shape=out_shapes,
        compiler_params=_cparams(("parallel",)),
    )(x, gain, w_p, *tables)


BISECT_GROUPS = 1


def _attn_prompt_tile(q_ref, qi_ref, wi_ref, k_ref, v_ref, ki_ref, ones_ref, tri_ref, o_ref, *, top_k, span):
    tq = q_ref.shape[0]
    n_chunks = span // LANES
    j = pl.program_id(1)
    qpos = j * tq + lax.broadcasted_iota(I32, (tq, LANES), 0)
    lane = lax.broadcasted_iota(I32, (tq, LANES), 1)

    ki = ki_ref[0:span, :]
    wi = wi_ref[...]
    score = jnp.zeros((tq, span), F32)
    for h in range(IDX_HEADS):
        s = lax.dot_general(qi_ref[:, h * IDX_DIM:(h + 1) * IDX_DIM], ki, NT_DIMS,
                            preferred_element_type=F32)
        score = score + jnp.maximum(s, 0.0) * wi[:, h:h + 1]
    causal = [(c * LANES + lane) <= qpos for c in range(n_chunks)]
    keys = [jnp.where(causal[c], score[:, c * LANES:(c + 1) * LANES], -jnp.inf) for c in range(n_chunks)]

    ones = ones_ref[...]

    def lane_total(x):
        return jnp.dot(x.astype(BF16), ones, preferred_element_type=F32)

    def count(cmp):
        cnt = jnp.zeros((tq, LANES), F32)
        for c in range(n_chunks):
            cnt = cnt + jnp.where(cmp(keys[c]), 1.0, 0.0)
        return lane_total(cnt)

    kf = float(top_k)
    rg = tq // BISECT_GROUPS

    def count_ge(g, t):
        cnt = jnp.zeros((rg, LANES), F32)
        for c in range(n_chunks):
            cnt = cnt + jnp.where(keys[c][g * rg:(g + 1) * rg] >= t, 1.0, 0.0)
        return lane_total(cnt)

    thr = jnp.concatenate(_kth_largest(count_ge, kf, (rg, LANES), BISECT_GROUPS), axis=0)

    need = kf - count(lambda kc: kc > thr)
    tri = tri_ref[...]
    offs = jnp.zeros((tq, LANES), F32)
    bias_chunks = []
    for c in range(n_chunks):
        eq_c = keys[c] == thr
        eq_b = jnp.where(eq_c, 1.0, 0.0).astype(BF16)
        prefix = jnp.dot(eq_b, tri, preferred_element_type=F32) + offs
        sel = (keys[c] > thr) | (eq_c & (prefix <= need))
        bias_chunks.append(jnp.where(sel & causal[c], 0.0, -jnp.inf))
        offs = offs + jnp.dot(eq_b, ones, preferred_element_type=F32)
    bias = jnp.concatenate(bias_chunks, axis=1)

    heads_per_group = N_HEADS // N_KV_HEADS
    for g in range(N_KV_HEADS):
        kg = k_ref[0:span, g * HEAD_DIM:(g + 1) * HEAD_DIM]
        vg = v_ref[0:span, g * HEAD_DIM:(g + 1) * HEAD_DIM]
        for r in range(heads_per_group):
            sl = slice((g * heads_per_group + r) * HEAD_DIM, (g * heads_per_group + r + 1) * HEAD_DIM)
            logits = lax.dot_general(q_ref[:, sl], kg, NT_DIMS, preferred_element_type=F32) + bias
            m = jnp.max(logits, axis=-1, keepdims=True)
            p = jnp.exp(logits - m)
            den = jnp.sum(p, axis=-1, keepdims=True)
            o = jnp.dot(p.astype(BF16), vg, preferred_element_type=F32)
            o_ref[:, sl] = (o / den).astype(o_ref.dtype)


def _attn_prompt_kernel(*refs, top_k, n_cls):
    tq = refs[0].shape[0]
    nq = refs[3].shape[0] // tq
    width = nq // n_cls
    j = pl.program_id(1)
    for cls in range(n_cls):
        @pl.when(j // width == cls)
        def _(cls=cls):
            _attn_prompt_tile(*refs, top_k=top_k, span=(cls + 1) * width * tq)


def _attn_prompt(q, qi, wi, kb, vb, kib, batch, seq):
    tq = LANES
    top_k = min(INDEX_TOPK, seq // 4)
    ones = jnp.ones((LANES, LANES), BF16)
    tri = jnp.asarray(np.arange(LANES)[:, None] <= np.arange(LANES)[None, :], BF16)
    nq = seq // tq
    n_cls = 4 if nq % 4 == 0 else (2 if nq % 2 == 0 else 1)
    qrow = lambda w: pl.BlockSpec((tq, w), lambda b, j: (b * nq + j, 0))
    kvrow = lambda w: pl.BlockSpec((seq, w), lambda b, j: (b, 0))
    return pl.pallas_call(
        functools.partial(_attn_prompt_kernel, top_k=top_k, n_cls=n_cls),
        grid=(batch, nq),
        in_specs=[qrow(ATTN_WIDTH), qrow(QI_WIDTH), qrow(LANES), kvrow(KV_WIDTH), kvrow(KV_WIDTH), kvrow(IDX_DIM),
                  _resident((LANES, LANES)), _resident((LANES, LANES))],
        out_specs=qrow(ATTN_WIDTH),
        out_shape=jax.ShapeDtypeStruct((batch * seq, ATTN_WIDTH), BF16),
        compiler_params=_cparams(("parallel", "parallel")),
    )(q, qi, wi, kb, vb, kib, ones, tri)


def _attn_sample_kernel(pt_ref, qg_ref, qi_ref, wcol_ref, knew_ref, vnew_ref, kinew_ref, *rest,
                        n_pages, top_k, past, dec_seq):
    del pt_ref
    k_pages = rest[:n_pages]
    v_pages = rest[n_pages:2 * n_pages]
    ki_pages = rest[2 * n_pages:3 * n_pages]
    tri_ref, ones_ref, o_ref, kn_s, vn_s, kin_s = rest[3 * n_pages:]
    page = k_pages[0].shape[0]
    n_chunks = n_pages + 1
    rows = DEC_SEQ_PAD

    for scr, new in ((kn_s, knew_ref), (vn_s, vnew_ref), (kin_s, kinew_ref)):
        scr[...] = jnp.zeros(scr.shape, scr.dtype)
        scr[0:dec_seq, :] = new[...]

    def chunk(pages, scr, c, g):
        if c < n_pages:
            return pages[c][:, g, :].astype(BF16)
        return scr[:, g * HEAD_DIM:(g + 1) * HEAD_DIM].astype(BF16)

    def ki_chunk(c):
        return (ki_pages[c] if c < n_pages else kin_s)[...].astype(BF16)

    row = lax.broadcasted_iota(I32, (rows, page), 0)
    lane = lax.broadcasted_iota(I32, (rows, page), 1)
    qpos = past + row

    qi = qi_ref[...]
    wcol = wcol_ref[...]
    keys, causals = [], []
    for c in range(n_chunks):
        s = lax.dot_general(qi, ki_chunk(c), NT_DIMS, preferred_element_type=F32)
        s = jnp.maximum(s, 0.0) * wcol
        score = s[0:rows]
        for h in range(1, IDX_HEADS):
            score = score + s[h * rows:(h + 1) * rows]
        causal_c = (c * page + lane) <= qpos
        causals.append(causal_c)
        keys.append(jnp.where(causal_c, score, -jnp.inf))

    def count(cmp):
        tot = jnp.zeros((rows, page), F32)
        for c in range(n_chunks):
            tot = tot + jnp.where(cmp(keys[c]), 1.0, 0.0)
        return jnp.broadcast_to(jnp.sum(tot, axis=1, keepdims=True), (rows, page))

    kf = float(top_k)
    thr = _kth_largest(lambda g, t: count(lambda kc: kc >= t), kf, (rows, page))[0]
    need = kf - count(lambda kc: kc > thr)

    tri = tri_ref[...]
    ones = ones_ref[...]
    offs = jnp.zeros((rows, page), F32)
    biases = []
    for c in range(n_chunks):
        eq_c = keys[c] == thr
        eq_f = jnp.where(eq_c, 1.0, 0.0)
        prefix = jnp.dot(eq_f, tri, preferred_element_type=F32) + offs
        sel = (keys[c] > thr) | (eq_c & (prefix <= need))
        b8 = jnp.where(sel & causals[c], 0.0, -jnp.inf)
        biases.append(jnp.concatenate([b8] * (N_HEADS // N_KV_HEADS), axis=0))
        offs = offs + jnp.dot(eq_f, ones, preferred_element_type=F32)

    heads_per_group = N_HEADS // N_KV_HEADS
    for g in range(N_KV_HEADS):
        qg = qg_ref[g]
        logits = [lax.dot_general(qg, chunk(k_pages, kn_s, c, g), NT_DIMS, preferred_element_type=F32)
                  + biases[c] for c in range(n_chunks)]
        m = logits[0]
        for c in range(1, n_chunks):
            m = jnp.maximum(m, logits[c])
        m = jnp.max(m, axis=1, keepdims=True)
        den = jnp.zeros(logits[0].shape, F32)
        o = jnp.zeros((heads_per_group * rows, HEAD_DIM), F32)
        for c in range(n_chunks):
            p = jnp.exp(logits[c] - m)
            den = den + p
            o = o + jnp.dot(p.astype(BF16), chunk(v_pages, vn_s, c, g), preferred_element_type=F32)
        o = o / jnp.sum(den, axis=1, keepdims=True)
        for r in range(heads_per_group):
            o_ref[g * heads_per_group + r] = o[r * rows:(r + 1) * rows, :]


def _attn_sample(q_s, qi_s, wi_s, k_s, v_s, ki_s, cache_k, cache_v, cache_ki, layer, page_table, dec_seq):
    db, n_pages = page_table.shape
    page = cache_k.shape[2]
    past = n_pages * page
    top_k = min(INDEX_TOPK, (past + dec_seq) // 4)
    hpg = N_HEADS // N_KV_HEADS
    pad = DEC_SEQ_PAD - dec_seq
    qg = q_s.reshape(db, dec_seq, N_KV_HEADS, hpg, HEAD_DIM).transpose(0, 2, 3, 1, 4)
    qg = jnp.pad(qg, ((0, 0), (0, 0), (0, 0), (0, pad), (0, 0))).reshape(db, N_KV_HEADS, hpg * DEC_SEQ_PAD, HEAD_DIM)
    qi = qi_s.reshape(db, dec_seq, IDX_HEADS, IDX_DIM).transpose(0, 2, 1, 3)
    qi = jnp.pad(qi, ((0, 0), (0, 0), (0, pad), (0, 0))).reshape(db, IDX_HEADS * DEC_SEQ_PAD, IDX_DIM)
    wcol = wi_s[:, :IDX_HEADS].reshape(db, dec_seq, IDX_HEADS).transpose(0, 2, 1)
    wcol = jnp.pad(wcol, ((0, 0), (0, 0), (0, pad))).reshape(db, IDX_HEADS * DEC_SEQ_PAD, 1)
    wcol = jnp.broadcast_to(wcol, (db, IDX_HEADS * DEC_SEQ_PAD, page))
    knew = k_s.reshape(db, dec_seq, KV_WIDTH)
    vnew = v_s.reshape(db, dec_seq, KV_WIDTH)
    kinew = ki_s.reshape(db, dec_seq, IDX_DIM)
    tri = jnp.asarray(np.arange(page)[:, None] <= np.arange(page)[None, :], F32)
    ones = jnp.ones((page, page), F32)

    def seq_spec(shape):
        nd = len(shape)
        return pl.BlockSpec((None,) + tuple(shape[1:]), lambda b, pt: (b,) + (0,) * (nd - 1))

    def page_spec(tail, p):
        return pl.BlockSpec((None, None, page) + tail,
                            lambda b, pt: (layer, pt[b, p], 0) + (0,) * len(tail))

    kv_specs = [page_spec((N_KV_HEADS, HEAD_DIM), p) for p in range(n_pages)]

    const = lambda shape: pl.BlockSpec(shape, lambda b, pt: (0,) * len(shape))
    in_specs = [seq_spec(a.shape) for a in (qg, qi, wcol, knew, vnew, kinew)]
    in_specs += kv_specs + kv_specs + [page_spec((IDX_DIM,), p) for p in range(n_pages)]
    in_specs += [const((page, page)), const((page, page))]
    out = pl.pallas_call(
        functools.partial(_attn_sample_kernel, n_pages=n_pages, top_k=top_k, past=past, dec_seq=dec_seq),
        grid_spec=pltpu.PrefetchScalarGridSpec(
            num_scalar_prefetch=1,
            grid=(db,),
            in_specs=in_specs,
            out_specs=pl.BlockSpec((None, N_HEADS, DEC_SEQ_PAD, HEAD_DIM), lambda b, pt: (b, 0, 0, 0)),
            scratch_shapes=[pltpu.VMEM((page, KV_WIDTH), F32), pltpu.VMEM((page, KV_WIDTH), F32),
                            pltpu.VMEM((page, IDX_DIM), F32)],
        ),
        out_shape=jax.ShapeDtypeStruct((db, N_HEADS, DEC_SEQ_PAD, HEAD_DIM), F32),
        compiler_params=_cparams(("arbitrary",)),
    )(page_table, qg, qi, wcol, knew, vnew, kinew,
      *([cache_k] * len(kv_specs)), *([cache_v] * len(kv_specs)), *([cache_ki] * n_pages), tri, ones)
    out = out[:, :, :dec_seq, :].transpose(0, 2, 1, 3)
    return out.reshape(db * dec_seq, ATTN_WIDTH).astype(BF16)


def _ln_swish(y, g, b):
    mu = jnp.mean(y, axis=-1, keepdims=True)
    yc = y - mu
    var = jnp.mean(yc * yc, axis=-1, keepdims=True)
    yn = yc * lax.rsqrt(var + EPS) * g + b
    return yn * jax.nn.sigmoid(yn)


CONV_HALO = 32


def _conv_prompt_kernel(glu_ref, w_ref, cb_ref, g_ref, b_ref, o_ref, st_ref, seq_s, y_s):
    tt, ch = glu_ref.shape
    t = pl.program_id(1)

    @pl.when(t == 0)
    def _():
        seq_s[0:CONV_HALO, :] = jnp.zeros((CONV_HALO, ch), F32)

    @pl.when(t > 0)
    def _():
        seq_s[0:CONV_HALO, :] = seq_s[tt:tt + CONV_HALO, :]

    seq_s[CONV_HALO:CONV_HALO + tt, :] = glu_ref[...]

    rc, cc = 32, 256
    first = CONV_HALO - (CONV_WIDTH - 1)
    for c0 in range(0, ch, cc):
        for r0 in range(0, tt, rc):
            acc = jnp.zeros((rc, cc), F32)
            for j in range(CONV_WIDTH):
                lo = first + r0 + j
                acc = acc + seq_s[lo:lo + rc, c0:c0 + cc] * w_ref[j:j + 1, c0:c0 + cc]
            y_s[r0:r0 + rc, c0:c0 + cc] = acc
    o_ref[...] = _ln_swish(y_s[...] + cb_ref[...], g_ref[...], b_ref[...]).astype(o_ref.dtype)

    @pl.when(t == pl.num_programs(1) - 1)
    def _():
        st_ref[...] = seq_s[CONV_HALO + tt - (CONV_WIDTH - 1):CONV_HALO + tt, :]


def _conv_prompt(glu, conv_w, conv_b, ln_g, ln_b, batch, seq):
    ch = glu.shape[1]
    tt = LANES
    nt = seq // tt
    const = lambda shape: pl.BlockSpec(shape, lambda b, t: (0,) * len(shape))
    return pl.pallas_call(
        _conv_prompt_kernel,
        grid=(batch, nt),
        in_specs=[pl.BlockSpec((tt, ch), lambda b, t: (b * nt + t, 0)),
                  const((CONV_WIDTH, ch)), const((1, ch)), const((1, ch)), const((1, ch))],
        out_specs=[pl.BlockSpec((tt, ch), lambda b, t: (b * nt + t, 0)),
                   pl.BlockSpec((None, CONV_WIDTH - 1, ch), lambda b, t: (b, 0, 0))],
        out_shape=[jax.ShapeDtypeStruct((batch * seq, ch), BF16),
                   jax.ShapeDtypeStruct((batch, CONV_WIDTH - 1, ch), F32)],
        scratch_shapes=[pltpu.VMEM((CONV_HALO + tt, ch), F32), pltpu.VMEM((tt, ch), F32)],
        compiler_params=_cparams(("parallel", "arbitrary")),
    )(glu, conv_w, conv_b, ln_g, ln_b)


def _conv_sample_kernel(glu_ref, prev_ref, wsh_ref, w_ref, cb_ref, g_ref, b_ref, o_ref, st_ref, y_s, *, dec_seq):
    nb = prev_ref.shape[0]
    keep = CONV_WIDTH - 1 - dec_seq
    for b in range(nb):
        prev = prev_ref[b]
        for t in range(dec_seq):
            y = jnp.sum(prev * wsh_ref[t], axis=0, keepdims=True)
            for u in range(t + 1):
                j = CONV_WIDTH - 1 - t + u
                y = y + glu_ref[b * dec_seq + u:b * dec_seq + u + 1, :] * w_ref[j:j + 1, :]
            y_s[b * dec_seq + t:b * dec_seq + t + 1, :] = y
        st_ref[b, 0:keep, :] = prev_ref[b, dec_seq:CONV_WIDTH - 1, :]
        st_ref[b, keep:CONV_WIDTH - 1, :] = glu_ref[b * dec_seq:(b + 1) * dec_seq, :]
    o_ref[...] = _ln_swish(y_s[...] + cb_ref[...], g_ref[...], b_ref[...]).astype(o_ref.dtype)


def _conv_sample(glu, prev, conv_w, conv_b, ln_g, ln_b, dec_seq):
    db = prev.shape[0]
    ch = glu.shape[1]
    nb = 8
    wsh = jnp.stack([jnp.pad(conv_w[:CONV_WIDTH - 1 - t], ((t, 0), (0, 0))) for t in range(dec_seq)])
    const = lambda shape: pl.BlockSpec(shape, lambda i: (0,) * len(shape))
    return pl.pallas_call(
        functools.partial(_conv_sample_kernel, dec_seq=dec_seq),
        grid=(db // nb,),
        in_specs=[pl.BlockSpec((nb * dec_seq, ch), lambda i: (i, 0)),
                  pl.BlockSpec((nb, CONV_WIDTH - 1, ch), lambda i: (i, 0, 0)),
                  const((dec_seq, CONV_WIDTH - 1, ch)), const((CONV_WIDTH, ch)),
                  const((1, ch)), const((1, ch)), const((1, ch))],
        out_specs=[pl.BlockSpec((nb * dec_seq, ch), lambda i: (i, 0)),
                   pl.BlockSpec((nb, CONV_WIDTH - 1, ch), lambda i: (i, 0, 0))],
        out_shape=[jax.ShapeDtypeStruct((db * dec_seq, ch), BF16),
                   jax.ShapeDtypeStruct((db, CONV_WIDTH - 1, ch), F32)],
        scratch_shapes=[pltpu.VMEM((nb * dec_seq, ch), F32)],
        compiler_params=_cparams(("parallel",)),
    )(glu, prev, wsh, conv_w, conv_b, ln_g, ln_b)


def _outproj_kernel(h_ref, a_ref, c_ref, wa_ref, wc_ref, o_ref):
    o_ref[...] = (h_ref[...]
                  + jnp.dot(a_ref[...], wa_ref[...], preferred_element_type=F32)
                  + jnp.dot(c_ref[...], wc_ref[...], preferred_element_type=F32))


def _outproj(h, attn, conv, w_attn, w_conv):
    t, d = h.shape
    tm = _token_tile(t)
    row = lambda w: pl.BlockSpec((tm, w), lambda i: (i, 0))
    return pl.pallas_call(
        _outproj_kernel,
        grid=(t // tm,),
        in_specs=[row(d), row(attn.shape[1]), row(conv.shape[1]), _resident(w_attn.shape), _resident(w_conv.shape)],
        out_specs=row(d),
        out_shape=jax.ShapeDtypeStruct((t, d), F32),
        compiler_params=_cparams(("parallel",)),
    )(h, attn, conv, w_attn, w_conv)


def _stair_table():
    k = PEER_TOPK
    rows = [b for b in range(k)]
    for a in range(1, 8):
        rows += [a * k + b if (a + 1) * (b + 1) <= k else -1 for b in range(8)]
    rows += [a * k for a in range(8, k)]
    return np.asarray(rows, np.int32)


def _top_rows(s, k, ids):
    big = np.int32(2 ** 30)
    out_rows = lax.broadcasted_iota(I32, (k, s.shape[1]), 0)
    vals = jnp.zeros((k, s.shape[1]), F32)
    sel_ids = jnp.zeros((k, s.shape[1]), I32)
    for r in range(k):
        m = jnp.max(s, axis=0, keepdims=True)
        first = jnp.min(jnp.where(s == m, ids, big), axis=0, keepdims=True)
        vals = jnp.where(out_rows == r, m, vals)
        sel_ids = jnp.where(out_rows == r, first, sel_ids)
        s = jnp.where(ids == first, -jnp.inf, s)
    return vals, sel_ids


def _peer_select_kernel(h_ref, g_ref, wqt_ref, sk_ref, flat_ref, xn_ref, eidx_ref, gate_ref):
    c = h_ref.shape[0]
    k = PEER_TOPK
    xn = _rmsnorm_rows(h_ref[...], g_ref[...])
    xn_ref[...] = xn
    qt = lax.dot_general(wqt_ref[...], xn.astype(BF16), NT_DIMS, preferred_element_type=F32)
    key_ids = lax.broadcasted_iota(I32, (PEER_KEYS, c), 0)
    flat = flat_ref[...]
    valid = flat >= 0
    e_rows, g_rows = [], []
    for h in range(PEER_HEADS):
        tops = []
        for n in range(2):
            hn = h * 2 + n
            qhn = qt[hn * PEER_KEYS:(hn + 1) * PEER_KEYS, :].astype(BF16)
            s = jnp.dot(sk_ref[hn], qhn, preferred_element_type=F32)
            tops.append(_top_rows(s, k, key_ids))
        (s1, i1), (s2, i2) = tops
        cv = [s1[0:1] + s2] + [s1[a:a + 1] + s2[0:8] for a in range(1, 8)] + [s1[8:k] + s2[0:1]]
        ce = ([i1[0:1] * PEER_KEYS + i2] + [i1[a:a + 1] * PEER_KEYS + i2[0:8] for a in range(1, 8)]
              + [i1[8:k] * PEER_KEYS + i2[0:1]])
        cand_v = jnp.where(valid, jnp.concatenate(cv, axis=0), -jnp.inf)
        cand_e = jnp.concatenate(ce, axis=0)
        big = np.int32(2 ** 30)
        out_rows = lax.broadcasted_iota(I32, (k, c), 0)
        best = jnp.zeros((k, c), F32)
        best_e = jnp.zeros((k, c), I32)
        for r in range(k):
            m = jnp.max(cand_v, axis=0, keepdims=True)
            first = jnp.min(jnp.where(cand_v == m, flat, big), axis=0, keepdims=True)
            hit = flat == first
            e = jnp.max(jnp.where(hit, cand_e, -1), axis=0, keepdims=True)
            best = jnp.where(out_rows == r, m, best)
            best_e = jnp.where(out_rows == r, e, best_e)
            cand_v = jnp.where(hit, -jnp.inf, cand_v)
        ex = jnp.exp(best - best[0:1])
        g_rows.append(ex / jnp.sum(ex, axis=0, keepdims=True))
        e_rows.append(best_e)
    eidx_ref[...] = jnp.concatenate(e_rows, axis=0).T
    gate_ref[...] = jnp.concatenate(g_rows, axis=0).T


def _peer_select(h, gain, wq_t, subkeys):
    t, d = h.shape
    c = _token_tile(t)
    n_sel = PEER_HEADS * PEER_TOPK
    flat = jnp.asarray(np.broadcast_to(_stair_table()[:, None], (_stair_table().shape[0], c)))
    row = lambda w: pl.BlockSpec((c, w), lambda i: (i, 0))
    return pl.pallas_call(
        _peer_select_kernel,
        grid=(t // c,),
        in_specs=[row(d), _resident((1, d)), _resident(wq_t.shape), _resident(subkeys.shape), _resident(flat.shape)],
        out_specs=[row(d), row(n_sel), row(n_sel)],
        out_shape=[jax.ShapeDtypeStruct((t, d), F32), jax.ShapeDtypeStruct((t, n_sel), I32),
                   jax.ShapeDtypeStruct((t, n_sel), F32)],
        compiler_params=_cparams(("parallel",)),
    )(h, gain, wq_t, subkeys, flat)


EXPERT_AHEAD = 10
EXPERT_SLOTS = EXPERT_AHEAD + 2
EXPERT_TOKEN_BLOCK = 384


def _split_hi_lo(x):
    hi = x.astype(BF16)
    return jnp.concatenate([hi, (x - hi.astype(F32)).astype(BF16)], axis=0)


def _peer_experts_kernel(eidx_hbm, xn_ref, gate_ref, h_ref, tbl_hbm, dup_ref, o_ref, idx_s, *rest):
    bufs = rest[:EXPERT_SLOTS]
    idx_sem, sem = rest[EXPERT_SLOTS:]
    tb, ns, _ = xn_ref.shape
    n_sel = gate_ref.shape[1]
    i = pl.program_id(0)
    cur = i % 2

    def idx_copy(step, slot):
        return pltpu.make_async_copy(eidx_hbm.at[pl.ds(step * tb, tb), :],
                                     idx_s.at[pl.ds(pl.multiple_of(slot * tb, 8), tb), :], idx_sem.at[slot])

    @pl.when(i == 0)
    def _():
        idx_copy(0, 0).start()

    idx_copy(i, cur).wait()

    @pl.when(i + 1 < pl.num_programs(0))
    def _():
        idx_copy(i + 1, 1 - cur).start()

    def issue(c, slot):
        for j in range(n_sel):
            src = tbl_hbm.at[pl.ds(pl.multiple_of(idx_s[cur * tb + c, j] * ns, ns), ns), :]
            pltpu.make_async_copy(src, bufs[slot].at[:, j, :], sem.at[slot]).start(priority=j % 2)

    def wait(slot):
        pltpu.make_async_copy(bufs[slot], bufs[slot], sem.at[slot]).wait()

    for c in range(EXPERT_AHEAD):
        issue(c, c)
    nw = ns // 2
    n2 = 2 * n_sel
    par1 = lax.broadcasted_iota(I32, (ns, n2), 1) % 2
    want1 = lax.broadcasted_iota(I32, (ns, n2), 0) - par1
    ones8 = jnp.ones((8, LANES), BF16)

    def pre_activation(c, slot):
        x = xn_ref[c]
        p = jnp.zeros((n_sel, LANES), F32)
        for r in range(nw):
            w = bufs[slot][r]
            lo = pltpu.bitcast(w << 16, F32)
            hi = pltpu.bitcast(w & np.int32(-65536), F32)
            p = p + lo * x[2 * r:2 * r + 1] + hi * x[2 * r + 1:2 * r + 2]
        t = lax.dot_general(ones8, _split_hi_lo(p), NT_DIMS, preferred_element_type=F32)
        return t[0:1, :n_sel] + t[0:1, n_sel:]

    def combine(c, slot, hpre):
        gelu = 0.5 * hpre * (1.0 + lax.erf(hpre * (2.0 ** -0.5)))
        act = jnp.broadcast_to(gelu * gate_ref[pl.ds(c, 1), :], (8, n_sel))
        d = jnp.dot(_split_hi_lo(act), dup_ref[...], preferred_element_type=F32)
        act2 = d[:8] + d[8:]
        act2 = jnp.concatenate([act2] * (ns // 8), axis=0)
        a_bd = jnp.concatenate([jnp.where(want1 == 2 * r, act2, 0.0) for r in range(nw)], axis=1)
        w = bufs[slot][nw:2 * nw].reshape(nw * n_sel, LANES)
        v = pltpu.bitcast(w, jnp.bfloat16).astype(BF16)
        o = jnp.dot(_split_hi_lo(a_bd), v, preferred_element_type=F32)
        o_ref[c] = h_ref[c] + o[:ns] + o[ns:]

    def region(c, slot, hpre_prev):
        wait(slot)
        issue(jnp.minimum(c + EXPERT_AHEAD, tb - 1), (slot + EXPERT_AHEAD) % EXPERT_SLOTS)
        hpre = pre_activation(c, slot)
        if hpre_prev is not None:
            combine(c - 1, (slot - 1) % EXPERT_SLOTS, hpre_prev)
        return hpre

    hpre = None
    for c in range(EXPERT_SLOTS):
        hpre = region(c, c, hpre)

    def group(g, hpre):
        for slot in range(EXPERT_SLOTS):
            hpre = region(g * EXPERT_SLOTS + slot, slot, hpre)
        return hpre

    hpre = lax.fori_loop(1, tb // EXPERT_SLOTS, group, hpre)
    combine(tb - 1, (tb - 1) % EXPERT_SLOTS, hpre)
    for slot in range(EXPERT_AHEAD):
        wait(slot)


def _peer_experts(eidx, xn, gate, h, table):
    t, d = h.shape
    n_sel = eidx.shape[1]
    ns = d // LANES
    tb = EXPERT_TOKEN_BLOCK
    assert t % tb == 0 and tb % EXPERT_SLOTS == 0 and tb >= 2 * EXPERT_SLOTS
    dup = jnp.asarray(np.arange(n_sel)[:, None] == np.arange(2 * n_sel)[None, :] // 2, BF16)
    row3 = pl.BlockSpec((tb, ns, LANES), lambda i: (i, 0, 0))
    out = pl.pallas_call(
        _peer_experts_kernel,
        grid=(t // tb,),
        in_specs=[pl.BlockSpec(memory_space=pl.ANY), row3, pl.BlockSpec((tb, n_sel), lambda i: (i, 0)), row3,
                  pl.BlockSpec(memory_space=pl.ANY), _resident((n_sel, 2 * n_sel))],
        out_specs=row3,
        out_shape=jax.ShapeDtypeStruct((t, ns, LANES), F32),
        scratch_shapes=([pltpu.SMEM((2 * tb, n_sel), I32)] + [pltpu.VMEM((ns, n_sel, LANES), I32)] * EXPERT_SLOTS
                        + [pltpu.SemaphoreType.DMA((2,)), pltpu.SemaphoreType.DMA((EXPERT_SLOTS,))]),
        compiler_params=_cparams(("arbitrary",)),
    )(eidx, xn.reshape(t, ns, LANES), gate, h.reshape(t, ns, LANES), table, dup)
    return out.reshape(t, d)


def _ple_kernel(h_ref, g_ref, wg_ref, p_ref, wp_ref, o_ref):
    h = h_ref[...]
    gate = jax.nn.sigmoid(jnp.dot(_rmsnorm_rows(h, g_ref[...]).astype(BF16), wg_ref[...],
                                  preferred_element_type=F32))
    o_ref[...] = h + gate * jnp.dot(p_ref[...].astype(BF16), wp_ref[...], preferred_element_type=F32)


def _final_norm_kernel(h_ref, g_ref, o_ref):
    o_ref[...] = _rmsnorm_rows(h_ref[...], g_ref[...])


def _ple(h, gain, w_gate, p, w_proj):
    t, d = h.shape
    tm = _token_tile(t)
    row = lambda w: pl.BlockSpec((tm, w), lambda i: (i, 0))
    return pl.pallas_call(
        _ple_kernel,
        grid=(t // tm,),
        in_specs=[row(d), _resident((1, d)), _resident(w_gate.shape), row(p.shape[1]), _resident(w_proj.shape)],
        out_specs=row(d),
        out_shape=jax.ShapeDtypeStruct((t, d), F32),
        compiler_params=_cparams(("parallel",)),
    )(h, gain, w_gate, p, w_proj)


def _final_norm(h, gain):
    t, d = h.shape
    tm = _token_tile(t)
    row = pl.BlockSpec((tm, d), lambda i: (i, 0))
    return pl.pallas_call(
        _final_norm_kernel,
        grid=(t // tm,),
        in_specs=[row, _resident((1, d))],
        out_specs=row,
        out_shape=jax.ShapeDtypeStruct((t, d), F32),
        compiler_params=_cparams(("parallel",)),
    )(h, gain)


def _rope_tables(pos, head_width):
    rot = head_width // ROT_FRACTION
    half = rot // 2
    inv = ROPE_THETA ** (-jnp.arange(half, dtype=F32) * 2.0 / rot)
    ang = pos.astype(F32)[:, None] * inv[None, :]
    cos, sin = jnp.cos(ang), jnp.sin(ang)
    n = pos.shape[0]
    zeros = lambda w: jnp.zeros((n, w), F32)
    c = jnp.concatenate([cos, cos, jnp.ones((n, head_width - rot), F32)], axis=1)
    m = jnp.concatenate([-sin, zeros(head_width - half)], axis=1)
    p = jnp.concatenate([zeros(half), sin, zeros(head_width - rot)], axis=1)
    reps = LANES // head_width
    return tuple(jnp.tile(a, (1, reps)) for a in (c, m, p))


def _pack_w_in(w_in, conv_ch):
    d = w_in.shape[0]
    o_k = ATTN_WIDTH
    o_v = o_k + KV_WIDTH
    o_qi = o_v + KV_WIDTH
    o_ki = o_qi + QI_WIDTH
    o_wi = o_ki + IDX_DIM
    o_glu = o_wi + IDX_HEADS
    z = lambda w: jnp.zeros((d, w), w_in.dtype)
    return jnp.concatenate([
        w_in[:, :o_ki], w_in[:, o_ki:o_wi], z(LANES - IDX_DIM), w_in[:, o_wi:o_glu], z(LANES - IDX_HEADS),
        w_in[:, o_glu:o_glu + 2 * conv_ch]], axis=1).astype(BF16)


def _pack_experts(u, v):
    n, d = u.shape
    nw = d // (2 * LANES)

    def rows(a):
        pairs = a.astype(jnp.bfloat16).reshape(n, nw, 2, LANES).transpose(0, 1, 3, 2)
        return lax.bitcast_convert_type(pairs, I32)

    return jnp.concatenate([rows(u), rows(v)], axis=1).reshape(n * 2 * nw, LANES)


def kernel(x_prompt, x_sample, cache_k, cache_v, cache_kidx, state_conv, page_table, p_prompt, p_sample, attn_norm, w_in, conv_w, conv_b, conv_ln_g, conv_ln_b, w_out, ffn_norm, peer_wq, peer_subkeys, peer_u, peer_v, ple_norm, ple_gate, ple_proj, final_norm):
    batch, seq, d = x_prompt.shape
    db, dec_seq, _ = x_sample.shape
    depth = attn_norm.shape[0]
    n_pool, page = cache_k.shape[1], cache_k.shape[2]
    n_pages = page_table.shape[1]
    past = n_pages * page
    conv_ch = d - ATTN_WIDTH
    tp = batch * seq
    assert seq % LANES == 0 and tp % LANES == 0 and (tp + db * dec_seq) % LANES == 0
    assert page == LANES and dec_seq <= DEC_SEQ_PAD and db % 8 == 0

    pos = jnp.concatenate([jnp.tile(jnp.arange(seq), batch), jnp.tile(past + jnp.arange(dec_seq), db)])
    tables = _rope_tables(pos, HEAD_DIM) + _rope_tables(pos, IDX_DIM)

    h = jnp.concatenate([x_prompt.reshape(tp, d), x_sample.reshape(db * dec_seq, d)], axis=0)
    outs = [[] for _ in range(8)]
    for i in range(depth):
        w_p = _pack_w_in(w_in[i], conv_ch)
        q, k, v, qi, ki, wi, glu, kb, vb, kib = _inproj(h, attn_norm[i][None], w_p, tables, conv_ch)

        attn_p = _attn_prompt(q, qi, wi, kb, vb, kib, batch, seq)
        attn_s = _attn_sample(q[tp:], qi[tp:], wi[tp:], k[tp:], v[tp:], ki[tp:],
                              cache_k, cache_v, cache_kidx, i, page_table, dec_seq)
        row = lambda a: a[None]
        conv_p, st_p = _conv_prompt(glu, conv_w[i], row(conv_b[i]), row(conv_ln_g[i]), row(conv_ln_b[i]),
                                    batch, seq)
        conv_s, st_s = _conv_sample(glu[tp:], state_conv[i], conv_w[i], row(conv_b[i]), row(conv_ln_g[i]),
                                    row(conv_ln_b[i]), dec_seq)
        w_o = w_out[i].astype(BF16)
        h = _outproj(h, jnp.concatenate([attn_p, attn_s], axis=0), jnp.concatenate([conv_p, conv_s], axis=0),
                     w_o[:ATTN_WIDTH], w_o[ATTN_WIDTH:])

        sk = peer_subkeys[i].reshape(PEER_HEADS * 2, PEER_KEYS, -1).astype(BF16)
        xn, eidx, gate = _peer_select(h, row(ffn_norm[i]), peer_wq[i].T.astype(BF16), sk)
        h = _peer_experts(eidx, xn, gate, h, _pack_experts(peer_u[i], peer_v[i]))

        p_all = jnp.concatenate([p_prompt[i].reshape(tp, -1), p_sample[i].reshape(db * dec_seq, -1)], axis=0)
        h = _ple(h, row(ple_norm[i]), ple_gate[i].astype(BF16), p_all, ple_proj[i].astype(BF16))

        outs[0].append(k[:tp].reshape(batch, seq, N_KV_HEADS, HEAD_DIM))
        outs[1].append(v[:tp].reshape(batch, seq, N_KV_HEADS, HEAD_DIM))
        outs[2].append(ki[:tp].reshape(batch, seq, IDX_DIM))
        outs[3].append(st_p)
        outs[4].append(k[tp:].reshape(db, dec_seq, N_KV_HEADS, HEAD_DIM))
        outs[5].append(v[tp:].reshape(db, dec_seq, N_KV_HEADS, HEAD_DIM))
        outs[6].append(ki[tp:].reshape(db, dec_seq, IDX_DIM))
        outs[7].append(st_s)

    y = _final_norm(h, final_norm[None])
    return (y[:tp].reshape(batch, seq, d), y[tp:].reshape(db, dec_seq, d)) + tuple(jnp.stack(o) for o in outs)
```

```python
import functools

import numpy as np
import jax
import jax.numpy as jnp
from jax import lax
from jax.experimental import pallas as pl
from jax.experimental.pallas import tpu as pltpu

N_HEADS = 8
HEAD_DIM = 128
N_KV_HEADS = 2
IDX_HEADS = 8
IDX_DIM = 64
INDEX_TOPK = 256
CONV_WIDTH = 31
ROPE_THETA = 500000.0
ROT_FRACTION = 4
PEER_HEADS = 8
PEER_KEYS = 128
PEER_TOPK = 16
EPS = 1e-6
DEC_SEQ_PAD = 8

LANES = 128
VMEM_LIMIT = 56 * 1024 * 1024

F32 = jnp.float32
BF16 = jnp.bfloat16
I32 = jnp.int32
INT_MIN = np.int32(-2 ** 31)
NT_DIMS = (((1,), (1,)), ((), ()))

ATTN_WIDTH = N_HEADS * HEAD_DIM
KV_WIDTH = N_KV_HEADS * HEAD_DIM
QI_WIDTH = IDX_HEADS * IDX_DIM
P_Q = 0
P_K = P_Q + ATTN_WIDTH
P_V = P_K + KV_WIDTH
P_QI = P_V + KV_WIDTH
P_KI = P_QI + QI_WIDTH
P_WI = P_KI + LANES
P_GLU = P_WI + LANES


def _cparams(sem=None):
    return pltpu.CompilerParams(dimension_semantics=sem, vmem_limit_bytes=VMEM_LIMIT)


def _resident(shape):
    nd = len(shape)
    return pl.BlockSpec(shape, lambda *_: (0,) * nd, pipeline_mode=pl.Buffered(1))


def _rmsnorm_rows(x, g):
    ms = jnp.mean(x * x, axis=-1, keepdims=True)
    return x * lax.rsqrt(ms + EPS) * g


def _code_to_float(code):
    s = code ^ INT_MIN
    return pltpu.bitcast(s ^ ((s >> 31) & np.int32(0x7FFFFFFF)), F32)


def _kth_largest(count_ge, k, shape, groups=1):
    def step(i, codes):
        bit = jnp.int32(1) << (31 - i)
        cands = [code | bit for code in codes]
        counts = [count_ge(g, _code_to_float(cands[g])) for g in range(groups)]
        return tuple(jnp.where(counts[g] >= k, cands[g], codes[g]) for g in range(groups))

    codes = lax.fori_loop(0, 32, step, tuple(jnp.zeros(shape, I32) for _ in range(groups)), unroll=2)
    return [jnp.where(code == 0, -jnp.inf, _code_to_float(code)) for code in codes]


def _token_tile(t):
    return 256 if t % 256 == 0 else 128


def _inproj_kernel(x_ref, g_ref, w_ref, c128, m128, p128, c64, m64, p64,
                   q_ref, k_ref, v_ref, qi_ref, ki_ref, wi_ref, glu_ref, kb_ref, vb_ref, kib_ref,
                   *, conv_ch):
    xn = _rmsnorm_rows(x_ref[...], g_ref[...]).astype(BF16)

    def mm(lo, hi):
        return jnp.dot(xn, w_ref[:, lo:hi], preferred_element_type=F32)

    def rope(z, c, m, p, half):
        return z * c + pltpu.roll(z, LANES - half, 1) * m + pltpu.roll(z, half, 1) * p

    rot_a = HEAD_DIM // ROT_FRACTION // 2
    rot_b = IDX_DIM // ROT_FRACTION // 2
    ca, ma, pa = c128[...], m128[...], p128[...]
    cb, mb, pb = c64[...], m64[...], p64[...]

    zq = mm(P_Q, P_K)
    for h in range(N_HEADS):
        sl = slice(h * HEAD_DIM, (h + 1) * HEAD_DIM)
        q_ref[:, sl] = (rope(zq[:, sl], ca, ma, pa, rot_a) * (HEAD_DIM ** -0.5)).astype(BF16)
    zk = mm(P_K, P_V)
    for h in range(N_KV_HEADS):
        sl = slice(h * HEAD_DIM, (h + 1) * HEAD_DIM)
        r = rope(zk[:, sl], ca, ma, pa, rot_a)
        k_ref[:, sl] = r
        kb_ref[:, sl] = r.astype(BF16)
    zv = mm(P_V, P_QI)
    v_ref[...] = zv
    vb_ref[...] = zv.astype(BF16)
    zqi = mm(P_QI, P_KI)
    for s in range(QI_WIDTH // LANES):
        sl = slice(s * LANES, (s + 1) * LANES)
        qi_ref[:, sl] = (rope(zqi[:, sl], cb, mb, pb, rot_b) * (IDX_DIM ** -0.5)).astype(BF16)
    r = rope(mm(P_KI, P_WI), cb, mb, pb, rot_b)[:, :IDX_DIM]
    ki_ref[...] = r
    kib_ref[...] = r.astype(BF16)
    wi_ref[...] = mm(P_WI, P_GLU) * (IDX_HEADS ** -0.5)
    za = mm(P_GLU, P_GLU + conv_ch)
    zb = mm(P_GLU + conv_ch, P_GLU + 2 * conv_ch)
    glu_ref[...] = za * jax.nn.sigmoid(zb)


def _inproj(x, gain, w_p, tables, conv_ch):
    t, d = x.shape
    tm = _token_tile(t)
    npad = w_p.shape[1]
    row = lambda w: pl.BlockSpec((tm, w), lambda i: (i, 0))
    out_shapes = [
        jax.ShapeDtypeStruct((t, ATTN_WIDTH), BF16),
        jax.ShapeDtypeStruct((t, KV_WIDTH), F32),
        jax.ShapeDtypeStruct((t, KV_WIDTH), F32),
        jax.ShapeDtypeStruct((t, QI_WIDTH), BF16),
        jax.ShapeDtypeStruct((t, IDX_DIM), F32),
        jax.ShapeDtypeStruct((t, LANES), F32),
        jax.ShapeDtypeStruct((t, conv_ch), F32),
        jax.ShapeDtypeStruct((t, KV_WIDTH), BF16),
        jax.ShapeDtypeStruct((t, KV_WIDTH), BF16),
        jax.ShapeDtypeStruct((t, IDX_DIM), BF16),
    ]
    return pl.pallas_call(
        functools.partial(_inproj_kernel, conv_ch=conv_ch),
        grid=(t // tm,),
        in_specs=[row(d), _resident((1, d)), _resident((d, npad))] + [row(LANES)] * 6,
        out_specs=[row(s.shape[1]) for s in out_shapes],
        out_shape=out_shapes,
        compiler_params=_cparams(("parallel",)),
    )(x, gain, w_p, *tables)


BISECT_GROUPS = 1


def _attn_prompt_tile(q_ref, qi_ref, wi_ref, k_ref, v_ref, ki_ref, ones_ref, tri_ref, o_ref, *, top_k, span):
    tq = q_ref.shape[0]
    n_chunks = span // LANES
    j = pl.program_id(1)
    qpos = j * tq + lax.broadcasted_iota(I32, (tq, LANES), 0)
    lane = lax.broadcasted_iota(I32, (tq, LANES), 1)

    ki = ki_ref[0:span, :]
    wi = wi_ref[...]
    score = jnp.zeros((tq, span), F32)
    for h in range(IDX_HEADS):
        s = lax.dot_general(qi_ref[:, h * IDX_DIM:(h + 1) * IDX_DIM], ki, NT_DIMS,
                            preferred_element_type=F32)
        score = score + jnp.maximum(s, 0.0) * wi[:, h:h + 1]
    causal = [(c * LANES + lane) <= qpos for c in range(n_chunks)]
    keys = [jnp.where(causal[c], score[:, c * LANES:(c + 1) * LANES], -jnp.inf) for c in range(n_chunks)]

    ones = ones_ref[...]

    def lane_total(x):
        return jnp.dot(x.astype(BF16), ones, preferred_element_type=F32)

    def count(cmp):
        cnt = jnp.zeros((tq, LANES), F32)
        for c in range(n_chunks):
            cnt = cnt + jnp.where(cmp(keys[c]), 1.0, 0.0)
        return lane_total(cnt)

    kf = float(top_k)
    rg = tq // BISECT_GROUPS

    def count_ge(g, t):
        cnt = jnp.zeros((rg, LANES), F32)
        for c in range(n_chunks):
            cnt = cnt + jnp.where(keys[c][g * rg:(g + 1) * rg] >= t, 1.0, 0.0)
        return lane_total(cnt)

    thr = jnp.concatenate(_kth_largest(count_ge, kf, (rg, LANES), BISECT_GROUPS), axis=0)

    need = kf - count(lambda kc: kc > thr)
    tri = tri_ref[...]
    offs = jnp.zeros((tq, LANES), F32)
    bias_chunks = []
    for c in range(n_chunks):
        eq_c = keys[c] == thr
        eq_b = jnp.where(eq_c, 1.0, 0.0).astype(BF16)
        prefix = jnp.dot(eq_b, tri, preferred_element_type=F32) + offs
        sel = (keys[c] > thr) | (eq_c & (prefix <= need))
        bias_chunks.append(jnp.where(sel & causal[c], 0.0, -jnp.inf))
        offs = offs + jnp.dot(eq_b, ones, preferred_element_type=F32)
    bias = jnp.concatenate(bias_chunks, axis=1)

    heads_per_group = N_HEADS // N_KV_HEADS
    for g in range(N_KV_HEADS):
        kg = k_ref[0:span, g * HEAD_DIM:(g + 1) * HEAD_DIM]
        vg = v_ref[0:span, g * HEAD_DIM:(g + 1) * HEAD_DIM]
        for r in range(heads_per_group):
            sl = slice((g * heads_per_group + r) * HEAD_DIM, (g * heads_per_group + r + 1) * HEAD_DIM)
            logits = lax.dot_general(q_ref[:, sl], kg, NT_DIMS, preferred_element_type=F32) + bias
            m = jnp.max(logits, axis=-1, keepdims=True)
            p = jnp.exp(logits - m)
            den = jnp.sum(p, axis=-1, keepdims=True)
            o = jnp.dot(p.astype(BF16), vg, preferred_element_type=F32)
            o_ref[:, sl] = (o / den).astype(o_ref.dtype)


def _attn_prompt_kernel(*refs, top_k, n_cls):
    tq = refs[0].shape[0]
    nq = refs[3].shape[0] // tq
    width = nq // n_cls
    j = pl.program_id(1)
    for cls in range(n_cls):
        @pl.when(j // width == cls)
        def _(cls=cls):
            _attn_prompt_tile(*refs, top_k=top_k, span=(cls + 1) * width * tq)


def _attn_prompt(q, qi, wi, kb, vb, kib, batch, seq):
    tq = LANES
    top_k = min(INDEX_TOPK, seq // 4)
    ones = jnp.ones((LANES, LANES), BF16)
    tri = jnp.asarray(np.arange(LANES)[:, None] <= np.arange(LANES)[None, :], BF16)
    nq = seq // tq
    n_cls = 4 if nq % 4 == 0 else (2 if nq % 2 == 0 else 1)
    qrow = lambda w: pl.BlockSpec((tq, w), lambda b, j: (b * nq + j, 0))
    kvrow = lambda w: pl.BlockSpec((seq, w), lambda b, j: (b, 0))
    return pl.pallas_call(
        functools.partial(_attn_prompt_kernel, top_k=top_k, n_cls=n_cls),
        grid=(batch, nq),
        in_specs=[qrow(ATTN_WIDTH), qrow(QI_WIDTH), qrow(LANES), kvrow(KV_WIDTH), kvrow(KV_WIDTH), kvrow(IDX_DIM),
                  _resident((LANES, LANES)), _resident((LANES, LANES))],
        out_specs=qrow(ATTN_WIDTH),
        out_shape=jax.ShapeDtypeStruct((batch * seq, ATTN_WIDTH), BF16),
        compiler_params=_cparams(("parallel", "parallel")),
    )(q, qi, wi, kb, vb, kib, ones, tri)


def _attn_sample_kernel(pt_ref, qg_ref, qi_ref, wcol_ref, knew_ref, vnew_ref, kinew_ref, *rest,
                        n_pages, top_k, past, dec_seq):
    del pt_ref
    k_pages = rest[:n_pages]
    v_pages = rest[n_pages:2 * n_pages]
    ki_pages = rest[2 * n_pages:3 * n_pages]
    tri_ref, ones_ref, o_ref, kn_s, vn_s, kin_s = rest[3 * n_pages:]
    page = k_pages[0].shape[0]
    n_chunks = n_pages + 1
    rows = DEC_SEQ_PAD

    for scr, new in ((kn_s, knew_ref), (vn_s, vnew_ref), (kin_s, kinew_ref)):
        scr[...] = jnp.zeros(scr.shape, scr.dtype)
        scr[0:dec_seq, :] = new[...]

    def chunk(pages, scr, c, g):
        if c < n_pages:
            return pages[c][:, g, :].astype(BF16)
        return scr[:, g * HEAD_DIM:(g + 1) * HEAD_DIM].astype(BF16)

    def ki_chunk(c):
        return (ki_pages[c] if c < n_pages else kin_s)[...].astype(BF16)

    row = lax.broadcasted_iota(I32, (rows, page), 0)
    lane = lax.broadcasted_iota(I32, (rows, page), 1)
    qpos = past + row

    qi = qi_ref[...]
    wcol = wcol_ref[...]
    keys, causals = [], []
    for c in range(n_chunks):
        s = lax.dot_general(qi, ki_chunk(c), NT_DIMS, preferred_element_type=F32)
        s = jnp.maximum(s, 0.0) * wcol
        score = s[0:rows]
        for h in range(1, IDX_HEADS):
            score = score + s[h * rows:(h + 1) * rows]
        causal_c = (c * page + lane) <= qpos
        causals.append(causal_c)
        keys.append(jnp.where(causal_c, score, -jnp.inf))

    def count(cmp):
        tot = jnp.zeros((rows, page), F32)
        for c in range(n_chunks):
            tot = tot + jnp.where(cmp(keys[c]), 1.0, 0.0)
        return jnp.broadcast_to(jnp.sum(tot, axis=1, keepdims=True), (rows, page))

    kf = float(top_k)
    thr = _kth_largest(lambda g, t: count(lambda kc: kc >= t), kf, (rows, page))[0]
    need = kf - count(lambda kc: kc > thr)

    tri = tri_ref[...]
    ones = ones_ref[...]
    offs = jnp.zeros((rows, page), F32)
    biases = []
    for c in range(n_chunks):
        eq_c = keys[c] == thr
        eq_f = jnp.where(eq_c, 1.0, 0.0)
        prefix = jnp.dot(eq_f, tri, preferred_element_type=F32) + offs
        sel = (keys[c] > thr) | (eq_c & (prefix <= need))
        b8 = jnp.where(sel & causals[c], 0.0, -jnp.inf)
        biases.append(jnp.concatenate([b8] * (N_HEADS // N_KV_HEADS), axis=0))
        offs = offs + jnp.dot(eq_f, ones, preferred_element_type=F32)

    heads_per_group = N_HEADS // N_KV_HEADS
    for g in range(N_KV_HEADS):
        qg = qg_ref[g]
        logits = [lax.dot_general(qg, chunk(k_pages, kn_s, c, g), NT_DIMS, preferred_element_type=F32)
                  + biases[c] for c in range(n_chunks)]
        m = logits[0]
        for c in range(1, n_chunks):
            m = jnp.maximum(m, logits[c])
        m = jnp.max(m, axis=1, keepdims=True)
        den = jnp.zeros(logits[0].shape, F32)
        o = jnp.zeros((heads_per_group * rows, HEAD_DIM), F32)
        for c in range(n_chunks):
            p = jnp.exp(logits[c] - m)
            den = den + p
            o = o + jnp.dot(p.astype(BF16), chunk(v_pages, vn_s, c, g), preferred_element_type=F32)
        o = o / jnp.sum(den, axis=1, keepdims=True)
        for r in range(heads_per_group):
            o_ref[g * heads_per_group + r] = o[r * rows:(r + 1) * rows, :]


def _attn_sample(q_s, qi_s, wi_s, k_s, v_s, ki_s, cache_k, cache_v, cache_ki, layer, page_table, dec_seq):
    db, n_pages = page_table.shape
    page = cache_k.shape[2]
    past = n_pages * page
    top_k = min(INDEX_TOPK, (past + dec_seq) // 4)
    hpg = N_HEADS // N_KV_HEADS
    pad = DEC_SEQ_PAD - dec_seq
    qg = q_s.reshape(db, dec_seq, N_KV_HEADS, hpg, HEAD_DIM).transpose(0, 2, 3, 1, 4)
    qg = jnp.pad(qg, ((0, 0), (0, 0), (0, 0), (0, pad), (0, 0))).reshape(db, N_KV_HEADS, hpg * DEC_SEQ_PAD, HEAD_DIM)
    qi = qi_s.reshape(db, dec_seq, IDX_HEADS, IDX_DIM).transpose(0, 2, 1, 3)
    qi = jnp.pad(qi, ((0, 0), (0, 0), (0, pad), (0, 0))).reshape(db, IDX_HEADS * DEC_SEQ_PAD, IDX_DIM)
    wcol = wi_s[:, :IDX_HEADS].reshape(db, dec_seq, IDX_HEADS).transpose(0, 2, 1)
    wcol = jnp.pad(wcol, ((0, 0), (0, 0), (0, pad))).reshape(db, IDX_HEADS * DEC_SEQ_PAD, 1)
    wcol = jnp.broadcast_to(wcol, (db, IDX_HEADS * DEC_SEQ_PAD, page))
    knew = k_s.reshape(db, dec_seq, KV_WIDTH)
    vnew = v_s.reshape(db, dec_seq, KV_WIDTH)
    kinew = ki_s.reshape(db, dec_seq, IDX_DIM)
    tri = jnp.asarray(np.arange(page)[:, None] <= np.arange(page)[None, :], F32)
    ones = jnp.ones((page, page), F32)

    def seq_spec(shape):
        nd = len(shape)
        return pl.BlockSpec((None,) + tuple(shape[1:]), lambda b, pt: (b,) + (0,) * (nd - 1))

    def page_spec(tail, p):
        return pl.BlockSpec((None, None, page) + tail,
                            lambda b, pt: (layer, pt[b, p], 0) + (0,) * len(tail))

    kv_specs = [page_spec((N_KV_HEADS, HEAD_DIM), p) for p in range(n_pages)]

    const = lambda shape: pl.BlockSpec(shape, lambda b, pt: (0,) * len(shape))
    in_specs = [seq_spec(a.shape) for a in (qg, qi, wcol, knew, vnew, kinew)]
    in_specs += kv_specs + kv_specs + [page_spec((IDX_DIM,), p) for p in range(n_pages)]
    in_specs += [const((page, page)), const((page, page))]
    out = pl.pallas_call(
        functools.partial(_attn_sample_kernel, n_pages=n_pages, top_k=top_k, past=past, dec_seq=dec_seq),
        grid_spec=pltpu.PrefetchScalarGridSpec(
            num_scalar_prefetch=1,
            grid=(db,),
            in_specs=in_specs,
            out_specs=pl.BlockSpec((None, N_HEADS, DEC_SEQ_PAD, HEAD_DIM), lambda b, pt: (b, 0, 0, 0)),
            scratch_shapes=[pltpu.VMEM((page, KV_WIDTH), F32), pltpu.VMEM((page, KV_WIDTH), F32),
                            pltpu.VMEM((page, IDX_DIM), F32)],
        ),
        out_shape=jax.ShapeDtypeStruct((db, N_HEADS, DEC_SEQ_PAD, HEAD_DIM), F32),
        compiler_params=_cparams(("arbitrary",)),
    )(page_table, qg, qi, wcol, knew, vnew, kinew,
      *([cache_k] * len(kv_specs)), *([cache_v] * len(kv_specs)), *([cache_ki] * n_pages), tri, ones)
    out = out[:, :, :dec_seq, :].transpose(0, 2, 1, 3)
    return out.reshape(db * dec_seq, ATTN_WIDTH).astype(BF16)


def _ln_swish(y, g, b):
    mu = jnp.mean(y, axis=-1, keepdims=True)
    yc = y - mu
    var = jnp.mean(yc * yc, axis=-1, keepdims=True)
    yn = yc * lax.rsqrt(var + EPS) * g + b
    return yn * jax.nn.sigmoid(yn)


CONV_HALO = 32


def _conv_prompt_kernel(glu_ref, w_ref, cb_ref, g_ref, b_ref, o_ref, st_ref, seq_s, y_s):
    tt, ch = glu_ref.shape
    t = pl.program_id(1)

    @pl.when(t == 0)
    def _():
        seq_s[0:CONV_HALO, :] = jnp.zeros((CONV_HALO, ch), F32)

    @pl.when(t > 0)
    def _():
        seq_s[0:CONV_HALO, :] = seq_s[tt:tt + CONV_HALO, :]

    seq_s[CONV_HALO:CONV_HALO + tt, :] = glu_ref[...]

    rc, cc = 32, 256
    first = CONV_HALO - (CONV_WIDTH - 1)
    for c0 in range(0, ch, cc):
        for r0 in range(0, tt, rc):
            acc = jnp.zeros((rc, cc), F32)
            for j in range(CONV_WIDTH):
                lo = first + r0 + j
                acc = acc + seq_s[lo:lo + rc, c0:c0 + cc] * w_ref[j:j + 1, c0:c0 + cc]
            y_s[r0:r0 + rc, c0:c0 + cc] = acc
    o_ref[...] = _ln_swish(y_s[...] + cb_ref[...], g_ref[...], b_ref[...]).astype(o_ref.dtype)

    @pl.when(t == pl.num_programs(1) - 1)
    def _():
        st_ref[...] = seq_s[CONV_HALO + tt - (CONV_WIDTH - 1):CONV_HALO + tt, :]


def _conv_prompt(glu, conv_w, conv_b, ln_g, ln_b, batch, seq):
    ch = glu.shape[1]
    tt = LANES
    nt = seq // tt
    const = lambda shape: pl.BlockSpec(shape, lambda b, t: (0,) * len(shape))
    return pl.pallas_call(
        _conv_prompt_kernel,
        grid=(batch, nt),
        in_specs=[pl.BlockSpec((tt, ch), lambda b, t: (b * nt + t, 0)),
                  const((CONV_WIDTH, ch)), const((1, ch)), const((1, ch)), const((1, ch))],
        out_specs=[pl.BlockSpec((tt, ch), lambda b, t: (b * nt + t, 0)),
                   pl.BlockSpec((None, CONV_WIDTH - 1, ch), lambda b, t: (b, 0, 0))],
        out_shape=[jax.ShapeDtypeStruct((batch * seq, ch), BF16),
                   jax.ShapeDtypeStruct((batch, CONV_WIDTH - 1, ch), F32)],
        scratch_shapes=[pltpu.VMEM((CONV_HALO + tt, ch), F32), pltpu.VMEM((tt, ch), F32)],
        compiler_params=_cparams(("parallel", "arbitrary")),
    )(glu, conv_w, conv_b, ln_g, ln_b)


def _conv_sample_kernel(glu_ref, prev_ref, wsh_ref, w_ref, cb_ref, g_ref, b_ref, o_ref, st_ref, y_s, *, dec_seq):
    nb = prev_ref.shape[0]
    keep = CONV_WIDTH - 1 - dec_seq
    for b in range(nb):
        prev = prev_ref[b]
        for t in range(dec_seq):
            y = jnp.sum(prev * wsh_ref[t], axis=0, keepdims=True)
            for u in range(t + 1):
                j = CONV_WIDTH - 1 - t + u
                y = y + glu_ref[b * dec_seq + u:b * dec_seq + u + 1, :] * w_ref[j:j + 1, :]
            y_s[b * dec_seq + t:b * dec_seq + t + 1, :] = y
        st_ref[b, 0:keep, :] = prev_ref[b, dec_seq:CONV_WIDTH - 1, :]
        st_ref[b, keep:CONV_WIDTH - 1, :] = glu_ref[b * dec_seq:(b + 1) * dec_seq, :]
    o_ref[...] = _ln_swish(y_s[...] + cb_ref[...], g_ref[...], b_ref[...]).astype(o_ref.dtype)


def _conv_sample(glu, prev, conv_w, conv_b, ln_g, ln_b, dec_seq):
    db = prev.shape[0]
    ch = glu.shape[1]
    nb = 8
    wsh = jnp.stack([jnp.pad(conv_w[:CONV_WIDTH - 1 - t], ((t, 0), (0, 0))) for t in range(dec_seq)])
    const = lambda shape: pl.BlockSpec(shape, lambda i: (0,) * len(shape))
    return pl.pallas_call(
        functools.partial(_conv_sample_kernel, dec_seq=dec_seq),
        grid=(db // nb,),
        in_specs=[pl.BlockSpec((nb * dec_seq, ch), lambda i: (i, 0)),
                  pl.BlockSpec((nb, CONV_WIDTH - 1, ch), lambda i: (i, 0, 0)),
                  const((dec_seq, CONV_WIDTH - 1, ch)), const((CONV_WIDTH, ch)),
                  const((1, ch)), const((1, ch)), const((1, ch))],
        out_specs=[pl.BlockSpec((nb * dec_seq, ch), lambda i: (i, 0)),
                   pl.BlockSpec((nb, CONV_WIDTH - 1, ch), lambda i: (i, 0, 0))],
        out_shape=[jax.ShapeDtypeStruct((db * dec_seq, ch), BF16),
                   jax.ShapeDtypeStruct((db, CONV_WIDTH - 1, ch), F32)],
        scratch_shapes=[pltpu.VMEM((nb * dec_seq, ch), F32)],
        compiler_params=_cparams(("parallel",)),
    )(glu, prev, wsh, conv_w, conv_b, ln_g, ln_b)


def _outproj_kernel(h_ref, a_ref, c_ref, wa_ref, wc_ref, o_ref):
    o_ref[...] = (h_ref[...]
                  + jnp.dot(a_ref[...], wa_ref[...], preferred_element_type=F32)
                  + jnp.dot(c_ref[...], wc_ref[...], preferred_element_type=F32))


def _outproj(h, attn, conv, w_attn, w_conv):
    t, d = h.shape
    tm = _token_tile(t)
    row = lambda w: pl.BlockSpec((tm, w), lambda i: (i, 0))
    return pl.pallas_call(
        _outproj_kernel,
        grid=(t // tm,),
        in_specs=[row(d), row(attn.shape[1]), row(conv.shape[1]), _resident(w_attn.shape), _resident(w_conv.shape)],
        out_specs=row(d),
        out_shape=jax.ShapeDtypeStruct((t, d), F32),
        compiler_params=_cparams(("parallel",)),
    )(h, attn, conv, w_attn, w_conv)


def _stair_table():
    k = PEER_TOPK
    rows = [b for b in range(k)]
    for a in range(1, 8):
        rows += [a * k + b if (a + 1) * (b + 1) <= k else -1 for b in range(8)]
    rows += [a * k for a in range(8, k)]
    return np.asarray(rows, np.int32)


def _top_rows(s, k, ids):
    big = np.int32(2 ** 30)
    out_rows = lax.broadcasted_iota(I32, (k, s.shape[1]), 0)
    vals = jnp.zeros((k, s.shape[1]), F32)
    sel_ids = jnp.zeros((k, s.shape[1]), I32)
    for r in range(k):
        m = jnp.max(s, axis=0, keepdims=True)
        first = jnp.min(jnp.where(s == m, ids, big), axis=0, keepdims=True)
        vals = jnp.where(out_rows == r, m, vals)
        sel_ids = jnp.where(out_rows == r, first, sel_ids)
        s = jnp.where(ids == first, -jnp.inf, s)
    return vals, sel_ids


def _peer_select_kernel(h_ref, g_ref, wqt_ref, sk_ref, flat_ref, eidx_ref, gate_ref):
    c = h_ref.shape[0]
    k = PEER_TOPK
    xn = _rmsnorm_rows(h_ref[...], g_ref[...])
    qt = lax.dot_general(wqt_ref[...], xn.astype(BF16), NT_DIMS, preferred_element_type=F32)
    key_ids = lax.broadcasted_iota(I32, (PEER_KEYS, c), 0)
    flat = flat_ref[...]
    valid = flat >= 0
    e_rows, g_rows = [], []
    for h in range(PEER_HEADS):
        tops = []
        for n in range(2):
            hn = h * 2 + n
            qhn = qt[hn * PEER_KEYS:(hn + 1) * PEER_KEYS, :].astype(BF16)
            s = jnp.dot(sk_ref[hn], qhn, preferred_element_type=F32)
            tops.append(_top_rows(s, k, key_ids))
        (s1, i1), (s2, i2) = tops
        cv = [s1[0:1] + s2] + [s1[a:a + 1] + s2[0:8] for a in range(1, 8)] + [s1[8:k] + s2[0:1]]
        ce = ([i1[0:1] * PEER_KEYS + i2] + [i1[a:a + 1] * PEER_KEYS + i2[0:8] for a in range(1, 8)]
              + [i1[8:k] * PEER_KEYS + i2[0:1]])
        cand_v = jnp.where(valid, jnp.concatenate(cv, axis=0), -jnp.inf)
        cand_e = jnp.concatenate(ce, axis=0)
        big = np.int32(2 ** 30)
        out_rows = lax.broadcasted_iota(I32, (k, c), 0)
        best = jnp.zeros((k, c), F32)
        best_e = jnp.zeros((k, c), I32)
        for r in range(k):
            m = jnp.max(cand_v, axis=0, keepdims=True)
            first = jnp.min(jnp.where(cand_v == m, flat, big), axis=0, keepdims=True)
            hit = flat == first
            e = jnp.max(jnp.where(hit, cand_e, -1), axis=0, keepdims=True)
            best = jnp.where(out_rows == r, m, best)
            best_e = jnp.where(out_rows == r, e, best_e)
            cand_v = jnp.where(hit, -jnp.inf, cand_v)
        ex = jnp.exp(best - best[0:1])
        g_rows.append(ex / jnp.sum(ex, axis=0, keepdims=True))
        e_rows.append(best_e)
    eidx_ref[...] = jnp.concatenate(e_rows, axis=0).T
    gate_ref[...] = jnp.concatenate(g_rows, axis=0).T


def _peer_select(h, gain, wq_t, subkeys):
    t, d = h.shape
    c = _token_tile(t)
    n_sel = PEER_HEADS * PEER_TOPK
    flat = jnp.asarray(np.broadcast_to(_stair_table()[:, None], (_stair_table().shape[0], c)))
    row = lambda w: pl.BlockSpec((c, w), lambda i: (i, 0))
    return pl.pallas_call(
        _peer_select_kernel,
        grid=(t // c,),
        in_specs=[row(d), _resident((1, d)), _resident(wq_t.shape), _resident(subkeys.shape), _resident(flat.shape)],
        out_specs=[row(n_sel), row(n_sel)],
        out_shape=[jax.ShapeDtypeStruct((t, n_sel), I32), jax.ShapeDtypeStruct((t, n_sel), F32)],
        compiler_params=_cparams(("parallel",)),
    )(h, gain, wq_t, subkeys, flat)


EXPERT_AHEAD = 10
EXPERT_SLOTS = EXPERT_AHEAD + 2
EXPERT_TOKEN_BLOCK = 384


def _split_hi_lo(x):
    hi = x.astype(BF16)
    return jnp.concatenate([hi, (x - hi.astype(F32)).astype(BF16)], axis=0)


def _peer_experts_kernel(eidx_hbm, norm_ref, gate_ref, h_ref, tbl_hbm, o_ref, idx_s, *rest):
    bufs = rest[:EXPERT_SLOTS]
    idx_sem, sem = rest[EXPERT_SLOTS:]
    tb, ns, _ = h_ref.shape
    n_sel = gate_ref.shape[1]
    i = pl.program_id(0)
    cur = i % 2

    def idx_copy(step, slot):
        return pltpu.make_async_copy(eidx_hbm.at[pl.ds(step * tb, tb), :],
                                     idx_s.at[pl.ds(pl.multiple_of(slot * tb, 8), tb), :], idx_sem.at[slot])

    @pl.when(i == 0)
    def _():
        idx_copy(0, 0).start()

    idx_copy(i, cur).wait()

    @pl.when(i + 1 < pl.num_programs(0))
    def _():
        idx_copy(i + 1, 1 - cur).start()

    def issue(c, slot):
        for j in range(n_sel):
            src = tbl_hbm.at[pl.ds(pl.multiple_of(idx_s[cur * tb + c, j] * ns, ns), ns), :]
            pltpu.make_async_copy(src, bufs[slot].at[:, j, :], sem.at[slot]).start(priority=j % 2)

    def wait(slot):
        pltpu.make_async_copy(bufs[slot], bufs[slot], sem.at[slot]).wait()

    for c in range(EXPERT_AHEAD):
        issue(c, c)
    diag = lax.broadcasted_iota(I32, (2 * ns, n_sel), 0) % ns
    row = lax.broadcasted_iota(I32, (ns, n_sel), 0)

    def words(slot):
        return bufs[slot][...].reshape(ns * n_sel, LANES)

    def pre_activation(c, slot):
        u = pltpu.bitcast(words(slot) & np.int32(-65536), F32).astype(BF16)
        x = h_ref[c]
        ms = jnp.sum(jnp.sum(x * x, axis=1, keepdims=True), axis=0, keepdims=True) * (1.0 / (ns * LANES))
        xn = x * lax.rsqrt(ms + EPS) * norm_ref[...]
        y = lax.dot_general(_split_hi_lo(xn), u, NT_DIMS, preferred_element_type=F32)
        acc = jnp.zeros((2 * ns, n_sel), F32)
        for s in range(ns):
            acc = acc + jnp.where(diag == s, y[:, s * n_sel:(s + 1) * n_sel], 0.0)
        return jnp.sum(acc, axis=0, keepdims=True)

    def combine(c, slot, hpre):
        gelu = 0.5 * hpre * (1.0 + lax.erf(hpre * (2.0 ** -0.5)))
        act = jnp.broadcast_to(gelu * gate_ref[pl.ds(c, 1), :], (ns, n_sel))
        a_bd = jnp.concatenate([jnp.where(row == s, act, 0.0) for s in range(ns)], axis=1)
        v = pltpu.bitcast(words(slot) << 16, F32).astype(BF16)
        o = jnp.dot(_split_hi_lo(a_bd), v, preferred_element_type=F32)
        o_ref[c] = h_ref[c] + o[:ns] + o[ns:]

    def region(c, slot, hpre_prev):
        wait(slot)
        issue(jnp.minimum(c + EXPERT_AHEAD, tb - 1), (slot + EXPERT_AHEAD) % EXPERT_SLOTS)
        hpre = pre_activation(c, slot)
        if hpre_prev is not None:
            combine(c - 1, (slot - 1) % EXPERT_SLOTS, hpre_prev)
        return hpre

    hpre = None
    for c in range(EXPERT_SLOTS):
        hpre = region(c, c, hpre)

    def group(g, hpre):
        for slot in range(EXPERT_SLOTS):
            hpre = region(g * EXPERT_SLOTS + slot, slot, hpre)
        return hpre

    hpre = lax.fori_loop(1, tb // EXPERT_SLOTS, group, hpre)
    combine(tb - 1, (tb - 1) % EXPERT_SLOTS, hpre)
    for slot in range(EXPERT_AHEAD):
        wait(slot)


def _peer_experts(eidx, norm_gain, gate, h, table):
    t, d = h.shape
    n_sel = eidx.shape[1]
    ns = d // LANES
    tb = EXPERT_TOKEN_BLOCK
    assert t % tb == 0 and tb % EXPERT_SLOTS == 0 and tb >= 2 * EXPERT_SLOTS
    row3 = pl.BlockSpec((tb, ns, LANES), lambda i: (i, 0, 0))
    out = pl.pallas_call(
        _peer_experts_kernel,
        grid=(t // tb,),
        in_specs=[pl.BlockSpec(memory_space=pl.ANY), _resident((ns, LANES)),
                  pl.BlockSpec((tb, n_sel), lambda i: (i, 0)), row3, pl.BlockSpec(memory_space=pl.ANY)],
        out_specs=row3,
        out_shape=jax.ShapeDtypeStruct((t, ns, LANES), F32),
        scratch_shapes=([pltpu.SMEM((2 * tb, n_sel), I32)] + [pltpu.VMEM((ns, n_sel, LANES), I32)] * EXPERT_SLOTS
                        + [pltpu.SemaphoreType.DMA((2,)), pltpu.SemaphoreType.DMA((EXPERT_SLOTS,))]),
        compiler_params=_cparams(("arbitrary",)),
    )(eidx, norm_gain.reshape(ns, LANES), gate, h.reshape(t, ns, LANES), table.reshape(-1, LANES))
    return out.reshape(t, d)


def _ple_kernel(h_ref, g_ref, wg_ref, p_ref, wp_ref, fg_ref, o_ref, *, last):
    h = h_ref[...]
    gate = jax.nn.sigmoid(jnp.dot(_rmsnorm_rows(h, g_ref[...]).astype(BF16), wg_ref[...],
                                  preferred_element_type=F32))
    h = h + gate * jnp.dot(p_ref[...].astype(BF16), wp_ref[...], preferred_element_type=F32)
    o_ref[...] = _rmsnorm_rows(h, fg_ref[...]) if last else h


def _ple(h, gain, w_gate, p, w_proj, final_gain, last):
    t, d = h.shape
    tm = _token_tile(t)
    row = lambda w: pl.BlockSpec((tm, w), lambda i: (i, 0))
    return pl.pallas_call(
        functools.partial(_ple_kernel, last=last),
        grid=(t // tm,),
        in_specs=[row(d), _resident((1, d)), _resident(w_gate.shape), row(p.shape[1]), _resident(w_proj.shape),
                  _resident((1, d))],
        out_specs=row(d),
        out_shape=jax.ShapeDtypeStruct((t, d), F32),
        compiler_params=_cparams(("parallel",)),
    )(h, gain, w_gate, p, w_proj, final_gain)


def _rope_tables(pos, head_width):
    rot = head_width // ROT_FRACTION
    half = rot // 2
    inv = ROPE_THETA ** (-jnp.arange(half, dtype=F32) * 2.0 / rot)
    ang = pos.astype(F32)[:, None] * inv[None, :]
    cos, sin = jnp.cos(ang), jnp.sin(ang)
    n = pos.shape[0]
    zeros = lambda w: jnp.zeros((n, w), F32)
    c = jnp.concatenate([cos, cos, jnp.ones((n, head_width - rot), F32)], axis=1)
    m = jnp.concatenate([-sin, zeros(head_width - half)], axis=1)
    p = jnp.concatenate([zeros(half), sin, zeros(head_width - rot)], axis=1)
    reps = LANES // head_width
    return tuple(jnp.tile(a, (1, reps)) for a in (c, m, p))


def _pack_w_in(w_in, conv_ch):
    d = w_in.shape[0]
    o_k = ATTN_WIDTH
    o_v = o_k + KV_WIDTH
    o_qi = o_v + KV_WIDTH
    o_ki = o_qi + QI_WIDTH
    o_wi = o_ki + IDX_DIM
    o_glu = o_wi + IDX_HEADS
    z = lambda w: jnp.zeros((d, w), w_in.dtype)
    return jnp.concatenate([
        w_in[:, :o_ki], w_in[:, o_ki:o_wi], z(LANES - IDX_DIM), w_in[:, o_wi:o_glu], z(LANES - IDX_HEADS),
        w_in[:, o_glu:o_glu + 2 * conv_ch]], axis=1).astype(BF16)


def _pack_experts(u, v):
    n, d = u.shape
    shape = (n, d // LANES, LANES)
    pair = jnp.stack([v.reshape(shape).astype(jnp.bfloat16), u.reshape(shape).astype(jnp.bfloat16)], axis=-1)
    return lax.bitcast_convert_type(pair, I32).reshape(n * (d // LANES), LANES)


def kernel(x_prompt, x_sample, cache_k, cache_v, cache_kidx, state_conv, page_table, p_prompt, p_sample, attn_norm, w_in, conv_w, conv_b, conv_ln_g, conv_ln_b, w_out, ffn_norm, peer_wq, peer_subkeys, peer_u, peer_v, ple_norm, ple_gate, ple_proj, final_norm):
    batch, seq, d = x_prompt.shape
    db, dec_seq, _ = x_sample.shape
    depth = attn_norm.shape[0]
    n_pool, page = cache_k.shape[1], cache_k.shape[2]
    n_pages = page_table.shape[1]
    past = n_pages * page
    conv_ch = d - ATTN_WIDTH
    tp = batch * seq
    assert seq % LANES == 0 and tp % LANES == 0 and (tp + db * dec_seq) % LANES == 0
    assert page == LANES and dec_seq <= DEC_SEQ_PAD and db % 8 == 0

    pos = jnp.concatenate([jnp.tile(jnp.arange(seq), batch), jnp.tile(past + jnp.arange(dec_seq), db)])
    tables = _rope_tables(pos, HEAD_DIM) + _rope_tables(pos, IDX_DIM)

    h = jnp.concatenate([x_prompt.reshape(tp, d), x_sample.reshape(db * dec_seq, d)], axis=0)
    outs = [[] for _ in range(8)]
    for i in range(depth):
        w_p = _pack_w_in(w_in[i], conv_ch)
        q, k, v, qi, ki, wi, glu, kb, vb, kib = _inproj(h, attn_norm[i][None], w_p, tables, conv_ch)

        attn_p = _attn_prompt(q, qi, wi, kb, vb, kib, batch, seq)
        attn_s = _attn_sample(q[tp:], qi[tp:], wi[tp:], k[tp:], v[tp:], ki[tp:],
                              cache_k, cache_v, cache_kidx, i, page_table, dec_seq)
        row = lambda a: a[None]
        conv_p, st_p = _conv_prompt(glu, conv_w[i], row(conv_b[i]), row(conv_ln_g[i]), row(conv_ln_b[i]),
                                    batch, seq)
        conv_s, st_s = _conv_sample(glu[tp:], state_conv[i], conv_w[i], row(conv_b[i]), row(conv_ln_g[i]),
                                    row(conv_ln_b[i]), dec_seq)
        w_o = w_out[i].astype(BF16)
        h = _outproj(h, jnp.concatenate([attn_p, attn_s], axis=0), jnp.concatenate([conv_p, conv_s], axis=0),
                     w_o[:ATTN_WIDTH], w_o[ATTN_WIDTH:])

        sk = peer_subkeys[i].reshape(PEER_HEADS * 2, PEER_KEYS, -1).astype(BF16)
        eidx, gate = _peer_select(h, row(ffn_norm[i]), peer_wq[i].T.astype(BF16), sk)
        h = _peer_experts(eidx, ffn_norm[i], gate, h, _pack_experts(peer_u[i], peer_v[i]))

        p_all = jnp.concatenate([p_prompt[i].reshape(tp, -1), p_sample[i].reshape(db * dec_seq, -1)], axis=0)
        h = _ple(h, row(ple_norm[i]), ple_gate[i].astype(BF16), p_all, ple_proj[i].astype(BF16),
                 final_norm[None], last=(i == depth - 1))

        outs[0].append(k[:tp].reshape(batch, seq, N_KV_HEADS, HEAD_DIM))
        outs[1].append(v[:tp].reshape(batch, seq, N_KV_HEADS, HEAD_DIM))
        outs[2].append(ki[:tp].reshape(batch, seq, IDX_DIM))
        outs[3].append(st_p)
        outs[4].append(k[tp:].reshape(db, dec_seq, N_KV_HEADS, HEAD_DIM))
        outs[5].append(v[tp:].reshape(db, dec_seq, N_KV_HEADS, HEAD_DIM))
        outs[6].append(ki[tp:].reshape(db, dec_seq, IDX_DIM))
        outs[7].append(st_s)

    y = h
    return (y[:tp].reshape(batch, seq, d), y[tp:].reshape(db, dec_seq, d)) + tuple(jnp.stack(o) for o in outs)
```
